```python
import jax, jax.numpy as jnp
from jax import lax
import numpy as np

D_MODEL = 1024
BATCH = 16
SEQ = 4096
DEPTH = 2

GRID_W = 64
CTX_LEN = 256
N_MOD = 6
FNET_GROUPS = 8
FNET_GROUP_DIM = 64
FNET_WIDTH = FNET_GROUPS * FNET_GROUP_DIM
RET_HEADS = 8
RET_DK = 64
RET_DV = 64
RET_QK_WIDTH = RET_HEADS * RET_DK
RET_V_WIDTH = RET_HEADS * RET_DV
RET_CHUNK = 128
RET_DECAY_BASE = 5.0
IN_WIDTH = FNET_WIDTH + 2 * RET_QK_WIDTH + 2 * RET_V_WIDTH
IN_SPLITS = (FNET_WIDTH, FNET_WIDTH + RET_QK_WIDTH, FNET_WIDTH + 2 * RET_QK_WIDTH,
             FNET_WIDTH + 2 * RET_QK_WIDTH + RET_V_WIDTH)
MIX_WIDTH = FNET_WIDTH + RET_V_WIDTH
ROPE_BASE = 10000.0
N_EXPERTS = 32
TOP_K = 4
D_EXPERT = D_MODEL
SWIGLU_LIMIT = 7.0
SWIGLU_ALPHA = 1.702
MOE_BLOCK = 128
NORM_EPS = 1e-6

kernel_name = "hybrid_fnet_retention_moe_dit"


def rmsnorm(x, w):
    x32 = x.astype(jnp.float32)
    y = x32 * lax.rsqrt(jnp.mean(x32 * x32, axis=-1, keepdims=True) + NORM_EPS)
    return (y * w.astype(jnp.float32)).astype(x.dtype)


def modulate(h, shift, scale):
    return h * (1 + scale) + shift


def flip(t):
    return jnp.flip(t, axis=1)


def heads(t, d):
    return t.reshape(t.shape[0], t.shape[1], RET_HEADS, d)


def axial_rope_angles(rows):
    row = jnp.broadcast_to(jnp.arange(rows)[:, None], (rows, GRID_W)).reshape(-1).astype(jnp.float32)
    col = jnp.broadcast_to(jnp.arange(GRID_W)[None, :], (rows, GRID_W)).reshape(-1).astype(jnp.float32)
    n_freq = RET_DK // 4
    freqs = ROPE_BASE ** (-jnp.arange(n_freq, dtype=jnp.float32) / n_freq)
    return jnp.concatenate([row[:, None] * freqs, col[:, None] * freqs], axis=-1)


def apply_rope(t, ang):
    cos = jnp.cos(ang)[None, :, None, :]
    sin = jnp.sin(ang)[None, :, None, :]
    t1, t2 = jnp.split(t, 2, axis=-1)
    return jnp.concatenate([t1 * cos - t2 * sin, t1 * sin + t2 * cos], axis=-1).astype(t.dtype)


def fourier_mix(z):
    b, t, _ = z.shape
    zg = z.reshape(b, t, FNET_GROUPS, FNET_GROUP_DIM).astype(jnp.float32)
    y = jnp.fft.fft2(zg, axes=(1, 3), norm="ortho").real
    return y.reshape(b, t, FNET_WIDTH)


def retention_states(k, v, log_g, s0):
    b, t, h, dk = k.shape
    n = t // RET_CHUNK
    kc = k.reshape(b, n, RET_CHUNK, h, dk)
    vc = v.reshape(b, n, RET_CHUNK, h, v.shape[-1])
    j = jnp.arange(RET_CHUNK, dtype=jnp.float32)
    w_kv = jnp.exp((RET_CHUNK - 1 - j)[:, None] * log_g[None, :])
    chunk_kv = jnp.einsum('bnjhd,jh,bnjhe->nbhde', kc, w_kv, vc)
    g_chunk = jnp.exp(RET_CHUNK * log_g)[None, :, None, None]

    def step(s, kv):
        return g_chunk * s + kv, s

    s_final, s_before = lax.scan(step, s0, chunk_kv)
    return s_before, s_final


def retention_outputs(q, k, v, log_g, s_before):
    b, t, h, dk = q.shape
    n = t // RET_CHUNK
    qc = q.reshape(b, n, RET_CHUNK, h, dk)
    kc = k.reshape(b, n, RET_CHUNK, h, dk)
    vc = v.reshape(b, n, RET_CHUNK, h, v.shape[-1])
    pos = jnp.arange(RET_CHUNK)
    diff = pos[:, None] - pos[None, :]
    decay = jnp.where((diff >= 0)[None],
                      jnp.exp(jnp.maximum(diff, 0).astype(jnp.float32)[None] * log_g[:, None, None]),
                      0.0)
    scores = jnp.einsum('bnihd,bnjhd->bnhij', qc, kc) * decay
    o_intra = jnp.einsum('bnhij,bnjhe->bnihe', scores, vc)
    w_q = jnp.exp((pos.astype(jnp.float32) + 1.0)[:, None] * log_g[None, :])
    o_cross = jnp.einsum('bnihd,ih,nbhde->bnihe', qc, w_q, s_before)
    return (o_intra + o_cross).reshape(b, t, h, -1)


def mixer_merge(f_in, o_ret, g, ret_norm_w, w_out, dtype):
    b, t = o_ret.shape[:2]
    o = o_ret.astype(jnp.float32)
    mu = jnp.mean(o, axis=-1, keepdims=True)
    var = jnp.mean(jnp.square(o - mu), axis=-1, keepdims=True)
    on = ((o - mu) * lax.rsqrt(var + NORM_EPS)).reshape(b, t, RET_V_WIDTH)
    y = jax.nn.silu(g.astype(jnp.float32)) * on * ret_norm_w.astype(jnp.float32)
    cat = jnp.concatenate([fourier_mix(f_in), y], axis=-1).astype(dtype)
    return cat @ w_out


def moe_ffn(h, router_w, router_b, w1, b1, w2, b2):
    n_tok, d = h.shape
    n_assign = n_tok * TOP_K
    logits = (h @ router_w + router_b).astype(jnp.float32)
    top_vals, top_idx = lax.top_k(logits, TOP_K)
    gates = jax.nn.softmax(top_vals, axis=-1)
    flat_e = top_idx.reshape(-1)
    flat_tok = jnp.arange(n_assign) // TOP_K
    order = jnp.argsort(flat_e)
    sorted_e = flat_e[order]
    sorted_tok = flat_tok[order]
    sorted_gate = gates.reshape(-1)[order]
    group_sizes = jnp.bincount(flat_e, length=N_EXPERTS)
    padded_sizes = (group_sizes + MOE_BLOCK - 1) // MOE_BLOCK * MOE_BLOCK
    group_start = jnp.cumsum(group_sizes) - group_sizes
    padded_end = jnp.cumsum(padded_sizes)
    padded_start = padded_end - padded_sizes
    dest = padded_start[sorted_e] + (jnp.arange(n_assign) - group_start[sorted_e])
    n_blocks = -(-n_assign // MOE_BLOCK) + N_EXPERTS
    buf = jnp.zeros((n_blocks * MOE_BLOCK, d), h.dtype).at[dest].set(h[sorted_tok])
    block_e = jnp.minimum(jnp.searchsorted(padded_end, jnp.arange(n_blocks) * MOE_BLOCK, side='right'),
                          N_EXPERTS - 1)

    def expert_block(args):
        xb, e = args
        hb = xb @ w1[e] + b1[e]
        x_glu = jnp.minimum(hb[:, ::2], SWIGLU_LIMIT)
        x_lin = jnp.clip(hb[:, 1::2], -SWIGLU_LIMIT, SWIGLU_LIMIT)
        act = x_glu * jax.nn.sigmoid(SWIGLU_ALPHA * x_glu) * (x_lin + 1)
        return act @ w2[e] + b2[e]

    out = lax.map(expert_block, (buf.reshape(n_blocks, MOE_BLOCK, d), block_e)).reshape(-1, d)
    contrib = out[dest].astype(jnp.float32) * sorted_gate[:, None]
    return jax.ops.segment_sum(contrib, sorted_tok, num_segments=n_tok).astype(h.dtype)


def setup_inputs(seed: int = 0) -> dict:
    key = jax.random.key(seed)
    ks = jax.random.split(key, 19)
    f32 = jnp.float32
    nrm = lambda k, shape, s: jax.random.normal(k, shape, f32) * s
    return {
        "x": nrm(ks[0], (BATCH, SEQ, D_MODEL), 1.0),
        "c": nrm(ks[1], (BATCH, D_MODEL), 1.0),
        "ctx": nrm(ks[2], (BATCH, CTX_LEN, D_MODEL), 1.0),
        "c_ctx": nrm(ks[3], (D_MODEL,), 1.0),
        "mod_w": nrm(ks[4], (DEPTH, D_MODEL, N_MOD * D_MODEL), 0.5 * D_MODEL ** -0.5),
        "mod_b": nrm(ks[5], (DEPTH, N_MOD * D_MODEL), 0.01),
        "norm_w": 1.0 + nrm(ks[6], (DEPTH, 2, D_MODEL), 0.01),
        "w_in": nrm(ks[7], (DEPTH, D_MODEL, IN_WIDTH), D_MODEL ** -0.5),
        "w_out": nrm(ks[8], (DEPTH, MIX_WIDTH, D_MODEL), MIX_WIDTH ** -0.5),
        "ret_decay": RET_DECAY_BASE + jnp.arange(RET_HEADS, dtype=f32) + nrm(ks[9], (DEPTH, 2, RET_HEADS), 0.05),
        "ret_norm_w": 1.0 + nrm(ks[10], (DEPTH, RET_V_WIDTH), 0.01),
        "router_w": nrm(ks[11], (DEPTH, D_MODEL, N_EXPERTS), D_MODEL ** -0.5),
        "router_b": nrm(ks[12], (DEPTH, N_EXPERTS), 0.01),
        "expert_w1": nrm(ks[13], (DEPTH, N_EXPERTS, D_MODEL, 2 * D_EXPERT), D_MODEL ** -0.5),
        "expert_b1": nrm(ks[14], (DEPTH, N_EXPERTS, 2 * D_EXPERT), 0.01),
        "expert_w2": nrm(ks[15], (DEPTH, N_EXPERTS, D_EXPERT, D_MODEL), D_EXPERT ** -0.5),
        "expert_b2": nrm(ks[16], (DEPTH, N_EXPERTS, D_MODEL), 0.01),
        "final_norm_w": 1.0 + nrm(ks[17], (D_MODEL,), 0.01),
    }


def reference(x, c, ctx, c_ctx, mod_w, mod_b, norm_w, w_in, w_out, ret_decay, ret_norm_w,
              router_w, router_b, expert_w1, expert_b1, expert_w2, expert_b2, final_norm_w):
    b, n_lat, d = x.shape
    rows = n_lat // GRID_W
    ang = axial_rope_angles(rows)
    s0 = jnp.zeros((b, RET_HEADS, RET_DK, RET_DV), jnp.float32)
    q_scale = RET_DK ** -0.5
    for l in range(DEPTH):
        has_ctx_out = l < DEPTH - 1
        mod_x = jnp.split((jax.nn.silu(c) @ mod_w[l] + mod_b[l])[:, None, :], N_MOD, axis=-1)
        mod_c = jnp.split(jax.nn.silu(c_ctx) @ mod_w[l] + mod_b[l], N_MOD, axis=-1)
        log_g = jnp.log1p(-jnp.exp2(-ret_decay[l].astype(jnp.float32)))

        hx = modulate(rmsnorm(x, norm_w[l, 0]), mod_x[0], mod_x[1])
        hc = modulate(rmsnorm(ctx, norm_w[l, 0]), mod_c[0], mod_c[1])
        fx, qx, kx, vx, gx = jnp.split(hx @ w_in[l], IN_SPLITS, axis=-1)
        fc, qc, kc, vc, gc = jnp.split(hc @ w_in[l], IN_SPLITS, axis=-1)
        qx = apply_rope(heads(qx, RET_DK) * q_scale, ang)
        kx = apply_rope(heads(kx, RET_DK), ang)
        vx = heads(vx, RET_DV)
        qc, kc, vc = heads(qc, RET_DK) * q_scale, heads(kc, RET_DK), heads(vc, RET_DV)

        sb_cf, s_cf = retention_states(kc, vc, log_g[0], s0)
        sb_cb, s_cb = retention_states(flip(kc), flip(vc), log_g[1], s0)
        sb_xf, _ = retention_states(kx, vx, log_g[0], s_cf)
        sb_xb, _ = retention_states(flip(kx), flip(vx), log_g[1], s_cb)
        o_x = (retention_outputs(qx, kx, vx, log_g[0], sb_xf)
               + flip(retention_outputs(flip(qx), flip(kx), flip(vx), log_g[1], sb_xb)))
        x = x + mod_x[2] * mixer_merge(fx, o_x, gx, ret_norm_w[l], w_out[l], x.dtype)
        if has_ctx_out:
            o_c = (retention_outputs(qc, kc, vc, log_g[0], sb_cf)
                   + flip(retention_outputs(flip(qc), flip(kc), flip(vc), log_g[1], sb_cb)))
            ctx = ctx + mod_c[2] * mixer_merge(fc, o_c, gc, ret_norm_w[l], w_out[l], ctx.dtype)

        hx = modulate(rmsnorm(x, norm_w[l, 1]), mod_x[3], mod_x[4])
        moe_args = (router_w[l], router_b[l], expert_w1[l], expert_b1[l], expert_w2[l], expert_b2[l])
        if has_ctx_out:
            hc = modulate(rmsnorm(ctx, norm_w[l, 1]), mod_c[3], mod_c[4])
            n_ctx_tok = hc.shape[0] * hc.shape[1]
            y = moe_ffn(jnp.concatenate([hc.reshape(-1, d), hx.reshape(-1, d)], axis=0), *moe_args)
            ctx = ctx + mod_c[5] * y[:n_ctx_tok].reshape(ctx.shape)
            x = x + mod_x[5] * y[n_ctx_tok:].reshape(x.shape)
        else:
            x = x + mod_x[5] * moe_ffn(hx.reshape(-1, d), *moe_args).reshape(x.shape)
    return rmsnorm(x, final_norm_w)
```

```python
import functools
import math

import jax
import jax.numpy as jnp
from jax import lax
from jax.experimental import pallas as pl
from jax.experimental.pallas import tpu as pltpu

D_MODEL = 1024
N_MOD = 6
GRID_W = 64
FNET_GROUP_DIM = 64
FNET_WIDTH = 512
RET_HEADS = 8
RET_DK = 64
RET_WIDTH = 512
RET_CHUNK = 128
IN_WIDTH = 2560
ROPE_BASE = 10000.0
N_EXPERTS = 32
TOP_K = 4
SWIGLU_LIMIT = 7.0
SWIGLU_ALPHA = 1.702
NORM_EPS = 1e-6

LANES = 128
TOKEN_TILE = 256
EXPERT_ROWS = 512
MOD_ROWS_PAD = 8

F32 = jnp.float32
BF16 = jnp.bfloat16


def _dot(a, b):
    return jnp.dot(a, b, preferred_element_type=F32)


def _split_bf16(a):
    hi = a.astype(BF16)
    lo = (a - hi.astype(F32)).astype(BF16)
    return hi, lo


def _dot3(a, b):
    a_hi, a_lo = _split_bf16(a)
    b_hi, b_lo = _split_bf16(b)
    return _dot(a_hi, b_hi) + _dot(a_lo, b_hi) + _dot(a_hi, b_lo)


def _silu(a):
    return a * (1.0 / (1.0 + jnp.exp(-a)))


def _rmsnorm(x, w):
    ms = jnp.mean(x * x, axis=-1, keepdims=True)
    return x * lax.rsqrt(ms + NORM_EPS) * w


def _mod_kernel(cc_ref, w_ref, b_ref, o_ref):
    o_ref[0] = _dot3(_silu(cc_ref[...]), w_ref[0]) + b_ref[0]


def _modulation(cc, mod_w, mod_b):
    depth, d, width = mod_w.shape
    rows = cc.shape[0]
    tn = width // 4
    return pl.pallas_call(
        _mod_kernel,
        grid=(depth, width // tn),
        in_specs=[
            pl.BlockSpec((rows, d), lambda l, j: (0, 0)),
            pl.BlockSpec((1, d, tn), lambda l, j: (l, 0, j)),
            pl.BlockSpec((1, 1, tn), lambda l, j: (l, 0, j)),
        ],
        out_specs=pl.BlockSpec((1, rows, tn), lambda l, j: (l, 0, j)),
        out_shape=jax.ShapeDtypeStruct((depth, rows, width), F32),
        name="modulation",
    )(cc, mod_w, mod_b.reshape(depth, 1, width))


def _mod_row(mods_ref, row, k):
    return mods_ref[pl.ds(row, 1), pl.ds(k * D_MODEL, D_MODEL)]


def _premix_kernel(nx_tiles, nb, x_ref, mods_ref, nw_ref, win_ref, cs_ref, cos_ref, sin_ref,
                   a_ref, b_ref, q_ref, k_ref, v_ref, g_ref):
    b = pl.program_id(0)
    t = pl.program_id(1)
    row = jnp.where(t >= nx_tiles, nb, b)
    y = _rmsnorm(x_ref[0], nw_ref[...])
    h = (y * (1.0 + _mod_row(mods_ref, row, 1)) + _mod_row(mods_ref, row, 0)).astype(BF16)
    z = _dot(h, win_ref[...])
    f = z[:, :FNET_WIDTH].astype(BF16)
    ab = _dot(f, cs_ref[...])
    a_ref[0] = ab[:, :FNET_WIDTH].astype(BF16)
    b_ref[0] = ab[:, FNET_WIDTH:].astype(BF16)

    cos = cos_ref[...]
    sin = sin_ref[...]
    lane = lax.broadcasted_iota(jnp.int32, cos.shape, 1)
    first_half = (lane % RET_DK) < (RET_DK // 2)

    def rope(u):
        outs = []
        for j in range(RET_WIDTH // LANES):
            c = u[:, j * LANES:(j + 1) * LANES]
            partner = jnp.where(first_half,
                                pltpu.roll(c, LANES - RET_DK // 2, 1),
                                pltpu.roll(c, RET_DK // 2, 1))
            outs.append(c * cos + partner * sin)
        return jnp.concatenate(outs, axis=1)

    o = FNET_WIDTH
    q_ref[0] = rope(z[:, o:o + RET_WIDTH] * (RET_DK ** -0.5)).astype(BF16)
    k_ref[0] = rope(z[:, o + RET_WIDTH:o + 2 * RET_WIDTH]).astype(BF16)
    v_ref[0] = z[:, o + 2 * RET_WIDTH:o + 3 * RET_WIDTH].astype(BF16)
    g_ref[0] = z[:, o + 3 * RET_WIDTH:o + 4 * RET_WIDTH].astype(BF16)


def _premix(xc, mods, norm_w, w_in, cs, cos_t, sin_t, n_lat):
    nb, s, d = xc.shape
    nt = s // TOKEN_TILE
    tok = lambda w: pl.BlockSpec((1, TOKEN_TILE, w), lambda b, t: (b, t, 0))
    full = lambda a: pl.BlockSpec(a.shape, lambda b, t: (0,) * a.ndim)
    out = jax.ShapeDtypeStruct((nb, s, RET_WIDTH), BF16)
    return pl.pallas_call(
        functools.partial(_premix_kernel, n_lat // TOKEN_TILE, nb),
        grid=(nb, nt),
        in_specs=[tok(d), full(mods), full(norm_w), full(w_in), full(cs),
                  pl.BlockSpec((TOKEN_TILE, LANES), lambda b, t: (t, 0)),
                  pl.BlockSpec((TOKEN_TILE, LANES), lambda b, t: (t, 0))],
        out_specs=[tok(RET_WIDTH)] * 6,
        out_shape=[out] * 6,
        name="premix",
    )(xc, mods, norm_w, w_in, cs, cos_t, sin_t)


def _dft_kernel(scale, c_ref, sn_ref, a_ref, b_ref, o_ref):
    acc = _dot(c_ref[...], a_ref[0]) + _dot(sn_ref[...], b_ref[0])
    o_ref[0] = (acc * scale).astype(BF16)


def _time_dft(a, bm, n, row_off, out_rows):
    nb, s, w = a.shape
    idx = lax.broadcasted_iota(jnp.int32, (n, n), 0) * lax.broadcasted_iota(jnp.int32, (n, n), 1)
    ang = (idx % n).astype(F32) * (2.0 * math.pi / n)
    cmat = jnp.cos(ang).astype(BF16)
    snmat = (-jnp.sin(ang)).astype(BF16)
    tm = min(n, TOKEN_TILE)
    scale = 1.0 / math.sqrt(n * FNET_GROUP_DIM)
    blk = row_off // n
    return pl.pallas_call(
        functools.partial(_dft_kernel, scale),
        grid=(nb, n // tm),
        in_specs=[pl.BlockSpec((tm, n), lambda b, i: (i, 0)),
                  pl.BlockSpec((tm, n), lambda b, i: (i, 0)),
                  pl.BlockSpec((1, n, w), lambda b, i: (b, blk, 0)),
                  pl.BlockSpec((1, n, w), lambda b, i: (b, blk, 0))],
        out_specs=pl.BlockSpec((1, tm, w), lambda b, i: (b, i, 0)),
        out_shape=jax.ShapeDtypeStruct((nb, out_rows, w), BF16),
        name=f"time_dft_{n}",
    )(cmat, snmat, a, bm)


def _ret_kernel(n_xc, n_cc, lg_ref, q_ref, k_ref, v_ref, g_ref, nw_ref, o_ref, sb_ref):
    c = RET_CHUNK
    p = pl.program_id(1)
    lgf0, lgf1 = lg_ref[0, 2 * p], lg_ref[0, 2 * p + 1]
    lgb0, lgb1 = lg_ref[1, 2 * p], lg_ref[1, 2 * p + 1]
    lane = lax.broadcasted_iota(jnp.int32, (c, c), 1)
    rowi = lax.broadcasted_iota(jnp.int32, (c, c), 0)
    lo = lane < RET_DK
    rlo = rowi < RET_DK
    lgf_l = jnp.where(lo, lgf0, lgf1)
    lgb_l = jnp.where(lo, lgb0, lgb1)
    rf = rowi.astype(F32)
    wqf = jnp.exp((rf + 1.0) * lgf_l)
    wqb = jnp.exp((c - rf) * lgb_l)
    wkf = jnp.exp((c - 1.0 - rf) * lgf_l)
    wkb = jnp.exp(rf * lgb_l)
    gf = jnp.exp(c * jnp.where(rlo, lgf0, lgf1))
    gb = jnp.exp(c * jnp.where(rlo, lgb0, lgb1))
    bmask = rlo == lo
    i2 = lax.broadcasted_iota(jnp.int32, (c, 2 * c), 0)
    c2 = lax.broadcasted_iota(jnp.int32, (c, 2 * c), 1)
    second = c2 >= c
    diff = (i2 - jnp.where(second, c2 - c, c2)).astype(F32)
    dcat = jnp.where(diff > 0, jnp.exp(diff * jnp.where(second, lgf1, lgf0)),
                     jnp.where(diff < 0, jnp.exp(-diff * jnp.where(second, lgb1, lgb0)), 2.0))
    nw = nw_ref[...]

    def load(ref, off):
        return ref[0, pl.ds(off, c), :].astype(F32)

    def split_heads(u):
        return jnp.concatenate([jnp.where(lo, u, 0.0), jnp.where(lo, 0.0, u)], axis=0).astype(BF16)

    def kv_update(s, decay, wk, off):
        kw = (load(k_ref, off) * wk).astype(BF16)
        kv = lax.dot_general(kw, load(v_ref, off).astype(BF16), (((0,), (0,)), ((), ())),
                             preferred_element_type=F32)
        return decay * s + jnp.where(bmask, kv, 0.0)

    def emit(s_f, off, cidx):
        q = load(q_ref, off)
        scores = lax.dot_general(q.astype(BF16), split_heads(load(k_ref, off)),
                                 (((1,), (1,)), ((), ())), preferred_element_type=F32)
        lhs = jnp.concatenate([(scores * dcat).astype(BF16), (q * wqf).astype(BF16),
                               (q * wqb).astype(BF16)], axis=1)
        rhs = jnp.concatenate([split_heads(load(v_ref, off)), s_f.astype(BF16),
                               sb_ref[cidx].astype(BF16)], axis=0)
        o = _dot(lhs, rhs)
        inv = 1.0 / RET_DK
        s0 = jnp.sum(jnp.where(lo, o, 0.0), axis=-1, keepdims=True)
        s1 = jnp.sum(o, axis=-1, keepdims=True) - s0
        dlt = o - jnp.where(lo, s0, s1) * inv
        d2 = dlt * dlt
        v0 = jnp.sum(jnp.where(lo, d2, 0.0), axis=-1, keepdims=True)
        v1 = jnp.sum(d2, axis=-1, keepdims=True) - v0
        on = dlt * lax.rsqrt(jnp.where(lo, v0, v1) * inv + NORM_EPS)
        y = _silu(load(g_ref, off)) * on * nw
        o_ref[0, pl.ds(off, c), :] = y.astype(BF16)

    zero = jnp.zeros((c, c), F32)
    s = zero
    for cc in reversed(range(n_cc)):
        sb_ref[n_xc + cc] = s
        s = kv_update(s, gb, wkb, (n_xc + cc) * c)

    def bwd_body(i, s):
        n = n_xc - 1 - i
        sb_ref[n] = s
        return kv_update(s, gb, wkb, pl.multiple_of(n * c, c))

    lax.fori_loop(0, n_xc, bwd_body, s)

    s = zero
    for cc in range(n_cc):
        off = (n_xc + cc) * c
        emit(s, off, n_xc + cc)
        s = kv_update(s, gf, wkf, off)

    def fwd_body(n, s):
        off = pl.multiple_of(n * c, c)
        emit(s, off, n)
        return kv_update(s, gf, wkf, off)

    lax.fori_loop(0, n_xc, fwd_body, s)


def _retention(q, k, v, g, log_g, ret_norm_w, n_lat):
    nb, s, w = q.shape
    n_xc = n_lat // RET_CHUNK
    n_cc = (s - n_lat) // RET_CHUNK
    seq = pl.BlockSpec((1, s, LANES), lambda b, p: (b, 0, p))
    return pl.pallas_call(
        functools.partial(_ret_kernel, n_xc, n_cc),
        grid=(nb, w // LANES),
        in_specs=[pl.BlockSpec(memory_space=pltpu.SMEM), seq, seq, seq, seq,
                  pl.BlockSpec((1, LANES), lambda b, p: (0, p))],
        out_specs=seq,
        out_shape=jax.ShapeDtypeStruct((nb, s, w), BF16),
        scratch_shapes=[pltpu.VMEM((n_xc + n_cc, RET_CHUNK, RET_CHUNK), F32)],
        name="retention",
    )(log_g, q, k, v, g, ret_norm_w)


def _postmix_kernel(nx_tiles, nb, yf_ref, yr_ref, x_ref, mods_ref, nw_ref, wout_ref, rw_ref, rb_ref,
                    xo_ref, h_ref, rec_ref, cnt_ref, carry_ref):
    b = pl.program_id(0)
    t = pl.program_id(1)

    @pl.when((b == 0) & (t == 0))
    def _():
        carry_ref[...] = jnp.zeros_like(carry_ref)

    row = jnp.where(t >= nx_tiles, nb, b)
    cat = jnp.concatenate([yf_ref[0], yr_ref[0]], axis=1)
    xn = x_ref[0] + _mod_row(mods_ref, row, 2) * _dot(cat, wout_ref[...])
    xo_ref[0] = xn
    h = _rmsnorm(xn, nw_ref[...]) * (1.0 + _mod_row(mods_ref, row, 4)) + _mod_row(mods_ref, row, 3)
    h_ref[0] = h

    logits = _dot3(h, rw_ref[...]) + rb_ref[...]
    tile = logits.shape[0]
    lane = lax.broadcasted_iota(jnp.int32, logits.shape, 1)
    work = logits
    sels, vals = [], []
    for _ in range(TOP_K):
        m = jnp.max(work, axis=-1, keepdims=True)
        idx = jnp.min(jnp.where(work == m, lane, LANES), axis=-1, keepdims=True)
        sel = lane == idx
        sels.append(sel)
        vals.append(m)
        work = jnp.where(sel, -jnp.inf, work)
    exps = [jnp.exp(v - vals[0]) for v in vals]
    den = exps[0] + exps[1] + exps[2] + exps[3]

    onehot = jnp.zeros(logits.shape, F32)
    for sel in sels:
        onehot = onehot + sel.astype(F32)
    ri = lax.broadcasted_iota(jnp.int32, (tile, tile), 0)
    ci = lax.broadcasted_iota(jnp.int32, (tile, tile), 1)
    tri = (ri > ci).astype(BF16)
    base = _dot(tri, onehot.astype(BF16)) + carry_ref[...]
    lane_f = lane.astype(F32)
    rec = jnp.zeros(logits.shape, F32)
    for kk in range(TOP_K):
        e_idx = jnp.sum(jnp.where(sels[kk], lane_f, 0.0), axis=-1, keepdims=True)
        rank = jnp.sum(jnp.where(sels[kk], base, 0.0), axis=-1, keepdims=True)
        rec = jnp.where(lane == kk, e_idx, rec)
        rec = jnp.where(lane == TOP_K + kk, exps[kk] / den, rec)
        rec = jnp.where(lane == 2 * TOP_K + kk, rank, rec)
    rec_ref[0] = rec
    carry_ref[...] = carry_ref[...] + jnp.sum(onehot, axis=0, keepdims=True)
    cnt_ref[...] = carry_ref[...]


def _postmix(yf, yr, xc, mods, norm_w, w_out, router_w, router_b, n_lat):
    nb, s, d = xc.shape
    nt = s // TOKEN_TILE
    tok = lambda w: pl.BlockSpec((1, TOKEN_TILE, w), lambda b, t: (b, t, 0))
    full = lambda a: pl.BlockSpec(a.shape, lambda b, t: (0,) * a.ndim)
    return pl.pallas_call(
        functools.partial(_postmix_kernel, n_lat // TOKEN_TILE, nb),
        grid=(nb, nt),
        in_specs=[tok(FNET_WIDTH), tok(RET_WIDTH), tok(d), full(mods), full(norm_w), full(w_out),
                  full(router_w), full(router_b)],
        out_specs=[tok(d), tok(d), tok(LANES), pl.BlockSpec((1, LANES), lambda b, t: (0, 0))],
        out_shape=[jax.ShapeDtypeStruct((nb, s, d), F32), jax.ShapeDtypeStruct((nb, s, d), F32),
                   jax.ShapeDtypeStruct((nb, s, LANES), F32), jax.ShapeDtypeStruct((1, LANES), F32)],
        scratch_shapes=[pltpu.VMEM((1, LANES), F32)],
        compiler_params=pltpu.CompilerParams(dimension_semantics=("arbitrary", "arbitrary")),
        name="postmix_router",
    )(yf, yr, xc, mods, norm_w, w_out, router_w, router_b)


def _expert_kernel(be_ref, nu_ref, tok_ref, tokn_ref, slot_ref, h_hbm,
                   w1g_ref, w1l_ref, b1g_ref, b1l_ref, w2_ref, b2_ref, out_hbm,
                   xbuf, obuf, gsem, ssem):
    del be_ref
    b = pl.program_id(0)
    nu = nu_ref[0]
    slot = b % 2
    rows = xbuf.shape[1]
    unroll = 8

    def gather_start(idx_ref, s):
        def body(i, carry):
            for u in range(unroll):
                j = i * unroll + u
                pltpu.make_async_copy(h_hbm.at[pl.ds(idx_ref[0, 0, j], 1)],
                                      xbuf.at[s, pl.ds(j, 1)], gsem.at[s]).start()
            return carry
        lax.fori_loop(0, rows // unroll, body, 0)

    def gather_wait(s):
        pltpu.make_async_copy(h_hbm.at[pl.ds(0, rows)], xbuf.at[s], gsem.at[s]).wait()

    def scatter_start(s):
        def body(i, carry):
            for u in range(unroll):
                j = i * unroll + u
                pltpu.make_async_copy(obuf.at[s, pl.ds(j, 1)],
                                      out_hbm.at[pl.ds(slot_ref[0, 0, j], 1)], ssem.at[s]).start()
            return carry
        lax.fori_loop(0, rows // unroll, body, 0)

    def scatter_wait(s):
        pltpu.make_async_copy(obuf.at[s], out_hbm.at[pl.ds(0, rows)], ssem.at[s]).wait()

    @pl.when(b < nu)
    def _():
        @pl.when(b == 0)
        def _():
            gather_start(tok_ref, slot)

        @pl.when(b >= 2)
        def _():
            scatter_wait(slot)

        @pl.when(b + 1 < nu)
        def _():
            gather_start(tokn_ref, 1 - slot)

        gather_wait(slot)
        x = xbuf[slot].astype(BF16)
        x_glu = jnp.minimum(_dot(x, w1g_ref[0]) + b1g_ref[0], SWIGLU_LIMIT)
        x_lin = jnp.clip(_dot(x, w1l_ref[0]) + b1l_ref[0], -SWIGLU_LIMIT, SWIGLU_LIMIT)
        act = x_glu * (1.0 / (1.0 + jnp.exp(-SWIGLU_ALPHA * x_glu))) * (x_lin + 1.0)
        obuf[slot] = _dot(act.astype(BF16), w2_ref[0]) + b2_ref[0]
        scatter_start(slot)

        @pl.when(b == nu - 1)
        def _():
            scatter_wait(slot)

            @pl.when(b >= 1)
            def _():
                scatter_wait(1 - slot)

            obuf[0] = jnp.zeros(obuf.shape[1:], F32)
            n_scratch = out_hbm.shape[0] - 2 * rows
            for half in range(2):
                fill = pltpu.make_async_copy(obuf.at[0], out_hbm.at[pl.ds(n_scratch + half * rows, rows)],
                                             ssem.at[0])
                fill.start()
                fill.wait()


def _experts(h_rows, block_e, n_used, row_tok, row_slot, w1g, w1l, b1g, b1l, w2, b2, n_out_rows):
    n_blocks = block_e.shape[0]
    d = h_rows.shape[1]
    rows = EXPERT_ROWS
    smem_blk = lambda fn: pl.BlockSpec((1, 1, rows), fn, memory_space=pltpu.SMEM)
    wspec = lambda a: pl.BlockSpec((1,) + a.shape[1:], lambda i, be, nu: (be[i], 0, 0))
    grid_spec = pltpu.PrefetchScalarGridSpec(
        num_scalar_prefetch=2,
        grid=(n_blocks,),
        in_specs=[smem_blk(lambda i, be, nu: (i, 0, 0)),
                  smem_blk(lambda i, be, nu: (jnp.minimum(i + 1, n_blocks - 1), 0, 0)),
                  smem_blk(lambda i, be, nu: (i, 0, 0)),
                  pl.BlockSpec(memory_space=pl.ANY),
                  wspec(w1g), wspec(w1l), wspec(b1g), wspec(b1l), wspec(w2), wspec(b2)],
        out_specs=pl.BlockSpec(memory_space=pl.ANY),
        scratch_shapes=[pltpu.VMEM((2, rows, d), F32), pltpu.VMEM((2, rows, d), F32),
                        pltpu.SemaphoreType.DMA((2,)), pltpu.SemaphoreType.DMA((2,))],
    )
    return pl.pallas_call(
        _expert_kernel,
        grid_spec=grid_spec,
        out_shape=jax.ShapeDtypeStruct((n_out_rows, d), F32),
        compiler_params=pltpu.CompilerParams(dimension_semantics=("arbitrary",)),
        name="expert_ffn",
    )(block_e, n_used, row_tok, row_tok, row_slot, h_rows, w1g, w1l, b1g, b1l, w2, b2)


def _combine_kernel(nx_tiles, nb, final, x_ref, o4_ref, rec_ref, mods_ref, fnw_ref, out_ref):
    b = pl.program_id(0)
    t = pl.program_id(1)
    row = jnp.where(t >= nx_tiles, nb, b)
    rec = rec_ref[0]
    d = D_MODEL
    y = rec[:, TOP_K:TOP_K + 1] * o4_ref[:, 0:d]
    for kk in range(1, TOP_K):
        y = y + rec[:, TOP_K + kk:TOP_K + kk + 1] * o4_ref[:, kk * d:(kk + 1) * d]
    xn = x_ref[0] + _mod_row(mods_ref, row, 5) * y
    if final:
        xn = _rmsnorm(xn, fnw_ref[...])
    out_ref[0] = xn


def _combine(xn, out4, rec, mods, final_norm_w, n_lat, final):
    nb, s, d = xn.shape
    out_rows = n_lat if final else s
    nt = out_rows // TOKEN_TILE
    tiles_per_seq = s // TOKEN_TILE
    tok = lambda w: pl.BlockSpec((1, TOKEN_TILE, w), lambda b, t: (b, t, 0))
    full = lambda a: pl.BlockSpec(a.shape, lambda b, t: (0,) * a.ndim)
    return pl.pallas_call(
        functools.partial(_combine_kernel, n_lat // TOKEN_TILE, nb, final),
        grid=(nb, nt),
        in_specs=[tok(d), pl.BlockSpec((TOKEN_TILE, TOP_K * d), lambda b, t: (b * tiles_per_seq + t, 0)),
                  tok(LANES), full(mods), full(final_norm_w)],
        out_specs=tok(d),
        out_shape=jax.ShapeDtypeStruct((nb, out_rows, d), F32),
        name="combine_final" if final else "combine",
    )(xn, out4, rec, mods, final_norm_w)


def _rope_tables(n_lat, n_ctx):
    pos = jnp.arange(n_lat)
    rowp = (pos // GRID_W).astype(F32)
    colp = (pos % GRID_W).astype(F32)
    n_freq = RET_DK // 4
    freqs = ROPE_BASE ** (-jnp.arange(n_freq, dtype=F32) / n_freq)
    ang = jnp.concatenate([rowp[:, None] * freqs, colp[:, None] * freqs], axis=-1)
    cos = jnp.tile(jnp.cos(ang), (1, LANES // (RET_DK // 2)))
    sin = jnp.tile(jnp.concatenate([-jnp.sin(ang), jnp.sin(ang)], axis=-1), (1, LANES // RET_DK))
    cos = jnp.concatenate([cos, jnp.ones((n_ctx, LANES), F32)], axis=0)
    sin = jnp.concatenate([sin, jnp.zeros((n_ctx, LANES), F32)], axis=0)
    return cos, sin


def _channel_dft_matrix():
    n = FNET_GROUP_DIM
    groups = FNET_WIDTH // n
    idx = jnp.arange(n)[:, None] * jnp.arange(n)[None, :]
    ang = (idx % n).astype(F32) * (2.0 * math.pi / n)
    eye = jnp.eye(groups, dtype=F32)
    return jnp.concatenate([jnp.kron(eye, jnp.cos(ang)), jnp.kron(eye, jnp.sin(ang))], axis=1).astype(BF16)


def _route_plan(rec, cnt, n_tok):
    rows = EXPERT_ROWS
    n_assign = n_tok * TOP_K
    n_blocks = n_assign // rows + N_EXPERTS
    rec2 = rec.reshape(n_tok, LANES)
    e_idx = rec2[:, 0:TOP_K].astype(jnp.int32)
    rank = rec2[:, 2 * TOP_K:3 * TOP_K].astype(jnp.int32)
    sizes = cnt[0, :N_EXPERTS].astype(jnp.int32)
    padded = (sizes + rows - 1) // rows * rows
    pend = jnp.cumsum(padded)
    pstart = pend - padded
    dest = (pstart[e_idx] + rank).reshape(-1)
    n_used = (pend[-1] // rows).astype(jnp.int32)
    blk = jnp.arange(n_blocks, dtype=jnp.int32)
    block_e = jnp.minimum(jnp.searchsorted(pend, blk * rows, side="right"), N_EXPERTS - 1).astype(jnp.int32)
    block_e = jnp.where(blk < n_used, block_e, block_e[jnp.maximum(n_used - 1, 0)])
    flat = jnp.arange(n_assign, dtype=jnp.int32)
    r = jnp.arange(n_blocks * rows, dtype=jnp.int32)
    pad_slot = n_assign + ((r // rows) % 2) * rows + r % rows
    row_tok = jnp.zeros((n_blocks * rows,), jnp.int32).at[dest].set(flat // TOP_K)
    row_slot = pad_slot.at[dest].set(flat)
    return (block_e, n_used.reshape(1), row_tok.reshape(n_blocks, 1, rows),
            row_slot.reshape(n_blocks, 1, rows), n_assign + 2 * rows)


def kernel(x, c, ctx, c_ctx, mod_w, mod_b, norm_w, w_in, w_out, ret_decay, ret_norm_w, router_w, router_b,
           expert_w1, expert_b1, expert_w2, expert_b2, final_norm_w):
    nb, n_lat, d = x.shape
    n_ctx = ctx.shape[1]
    depth = mod_w.shape[0]
    s = n_lat + n_ctx
    n_tok = nb * s

    mod_rows = -(-(nb + 1) // MOD_ROWS_PAD) * MOD_ROWS_PAD
    cc = jnp.concatenate([c, c_ctx[None, :], jnp.zeros((mod_rows - nb - 1, d), F32)], axis=0)
    mods_all = _modulation(cc, mod_w, mod_b)

    cos_t, sin_t = _rope_tables(n_lat, n_ctx)
    cs = _channel_dft_matrix()
    log_g = jnp.log1p(-jnp.exp2(-ret_decay.astype(F32)))
    rw_pad = jnp.pad(router_w, ((0, 0), (0, 0), (0, LANES - N_EXPERTS)))
    rb_pad = jnp.pad(router_b, ((0, 0), (0, LANES - N_EXPERTS)), constant_values=-1e30)

    xc = jnp.concatenate([x, ctx], axis=1)
    out = None
    for l in range(depth):
        final = l == depth - 1
        mods = mods_all[l]
        a, bm, q, k, v, g = _premix(xc, mods, norm_w[l, 0][None, :], w_in[l].astype(BF16), cs,
                                    cos_t, sin_t, n_lat)
        yf_x = _time_dft(a, bm, n_lat, 0, n_lat)
        yf_c = _time_dft(a, bm, n_ctx, n_lat, n_ctx)
        yf = jnp.concatenate([yf_x, yf_c], axis=1)
        yr = _retention(q, k, v, g, log_g[l], ret_norm_w[l][None, :], n_lat)
        xn, h, rec, cnt = _postmix(yf, yr, xc, mods, norm_w[l, 1][None, :], w_out[l].astype(BF16),
                                   rw_pad[l], rb_pad[l][None, :], n_lat)

        block_e, n_used, row_tok, row_slot, n_out_rows = _route_plan(rec, cnt, n_tok)
        w1 = expert_w1[l]
        b1 = expert_b1[l]
        out_rows = _experts(h.reshape(n_tok, d), block_e, n_used, row_tok, row_slot,
                            w1[:, :, 0::2].astype(BF16), w1[:, :, 1::2].astype(BF16),
                            b1[:, None, 0::2], b1[:, None, 1::2],
                            expert_w2[l].astype(BF16), expert_b2[l][:, None, :], n_out_rows)
        out4 = out_rows.reshape(n_out_rows // TOP_K, TOP_K * d)
        res = _combine(xn, out4, rec, mods, final_norm_w[None, :], n_lat, final)
        if final:
            out = res
        else:
            xc = res
    return out
```

```python
import functools
import math

import jax
import jax.numpy as jnp
from jax import lax
from jax.experimental import pallas as pl
from jax.experimental.pallas import tpu as pltpu

D_MODEL = 1024
N_MOD = 6
GRID_W = 64
FNET_GROUP_DIM = 64
FNET_WIDTH = 512
RET_HEADS = 8
RET_DK = 64
RET_WIDTH = 512
RET_CHUNK = 128
IN_WIDTH = 2560
ROPE_BASE = 10000.0
N_EXPERTS = 32
TOP_K = 4
SWIGLU_LIMIT = 7.0
SWIGLU_ALPHA = 1.702
NORM_EPS = 1e-6

LANES = 128
TOKEN_TILE = 256
EXPERT_ROWS = 512
MOD_ROWS_PAD = 8

F32 = jnp.float32
BF16 = jnp.bfloat16


def _dot(a, b):
    return jnp.dot(a, b, preferred_element_type=F32)


def _split_bf16(a):
    hi = a.astype(BF16)
    lo = (a - hi.astype(F32)).astype(BF16)
    return hi, lo


def _dot3(a, b):
    a_hi, a_lo = _split_bf16(a)
    b_hi, b_lo = _split_bf16(b)
    return _dot(a_hi, b_hi) + _dot(a_lo, b_hi) + _dot(a_hi, b_lo)


def _silu(a):
    return a * (1.0 / (1.0 + jnp.exp(-a)))


def _rmsnorm(x, w):
    ms = jnp.mean(x * x, axis=-1, keepdims=True)
    return x * lax.rsqrt(ms + NORM_EPS) * w


SLAB = D_MODEL // LANES


def _slab_read(ref, rows, lead=()):
    return jnp.concatenate([ref[lead + (pl.ds(s, rows, stride=SLAB), slice(None))] for s in range(SLAB)], axis=1)


def _slab_write(ref, val, col0=0, lead=()):
    rows = val.shape[0]
    for u in range(val.shape[1] // LANES):
        ref[lead + (pl.ds(col0 // LANES + u, rows, stride=SLAB), slice(None))] = val[:, u * LANES:(u + 1) * LANES]


def _mod_kernel(cc_ref, w_ref, b_ref, o_ref):
    o_ref[0] = _dot3(_silu(cc_ref[...]), w_ref[0]) + b_ref[0]


def _modulation(cc, mod_w, mod_b):
    depth, d, width = mod_w.shape
    rows = cc.shape[0]
    tn = width // 4
    return pl.pallas_call(
        _mod_kernel,
        grid=(depth, width // tn),
        in_specs=[
            pl.BlockSpec((rows, d), lambda l, j: (0, 0)),
            pl.BlockSpec((1, d, tn), lambda l, j: (l, 0, j)),
            pl.BlockSpec((1, 1, tn), lambda l, j: (l, 0, j)),
        ],
        out_specs=pl.BlockSpec((1, rows, tn), lambda l, j: (l, 0, j)),
        out_shape=jax.ShapeDtypeStruct((depth, rows, width), F32),
        name="modulation",
    )(cc, mod_w, mod_b.reshape(depth, 1, width))


def _mod_row(mods_ref, row, k):
    return mods_ref[pl.ds(row, 1), pl.ds(k * D_MODEL, D_MODEL)]


def _premix_kernel(nx_tiles, nb, x_ref, mods_ref, nw_ref, win_ref, cs_ref, cos_ref, sin_ref,
                   a_ref, b_ref, q_ref, k_ref, v_ref, g_ref):
    b = pl.program_id(0)
    t = pl.program_id(1)
    row = jnp.where(t >= nx_tiles, nb, b)
    y = _rmsnorm(x_ref[0], nw_ref[...])
    h = (y * (1.0 + _mod_row(mods_ref, row, 1)) + _mod_row(mods_ref, row, 0)).astype(BF16)
    z = _dot(h, win_ref[...])
    f = z[:, :FNET_WIDTH].astype(BF16)
    ab = _dot(f, cs_ref[...])
    a_ref[0] = ab[:, :FNET_WIDTH].astype(BF16)
    b_ref[0] = ab[:, FNET_WIDTH:].astype(BF16)

    cos = cos_ref[...]
    sin = sin_ref[...]
    lane = lax.broadcasted_iota(jnp.int32, cos.shape, 1)
    first_half = (lane % RET_DK) < (RET_DK // 2)

    def rope(u):
        outs = []
        for j in range(RET_WIDTH // LANES):
            c = u[:, j * LANES:(j + 1) * LANES]
            partner = jnp.where(first_half,
                                pltpu.roll(c, LANES - RET_DK // 2, 1),
                                pltpu.roll(c, RET_DK // 2, 1))
            outs.append(c * cos + partner * sin)
        return jnp.concatenate(outs, axis=1)

    o = FNET_WIDTH
    q_ref[0] = rope(z[:, o:o + RET_WIDTH] * (RET_DK ** -0.5)).astype(BF16)
    k_ref[0] = rope(z[:, o + RET_WIDTH:o + 2 * RET_WIDTH]).astype(BF16)
    v_ref[0] = z[:, o + 2 * RET_WIDTH:o + 3 * RET_WIDTH].astype(BF16)
    g_ref[0] = z[:, o + 3 * RET_WIDTH:o + 4 * RET_WIDTH].astype(BF16)


def _premix(xc, mods, norm_w, w_in, cs, cos_t, sin_t, n_lat):
    nb, s, d = xc.shape
    nt = s // TOKEN_TILE
    tok = lambda w: pl.BlockSpec((1, TOKEN_TILE, w), lambda b, t: (b, t, 0))
    full = lambda a: pl.BlockSpec(a.shape, lambda b, t: (0,) * a.ndim)
    out = jax.ShapeDtypeStruct((nb, s, RET_WIDTH), BF16)
    return pl.pallas_call(
        functools.partial(_premix_kernel, n_lat // TOKEN_TILE, nb),
        grid=(nb, nt),
        in_specs=[tok(d), full(mods), full(norm_w), full(w_in), full(cs),
                  pl.BlockSpec((TOKEN_TILE, LANES), lambda b, t: (t, 0)),
                  pl.BlockSpec((TOKEN_TILE, LANES), lambda b, t: (t, 0))],
        out_specs=[tok(RET_WIDTH)] * 6,
        out_shape=[out] * 6,
        name="premix",
    )(xc, mods, norm_w, w_in, cs, cos_t, sin_t)


def _dft_kernel(scale, c_ref, sn_ref, a_ref, b_ref, o_ref):
    acc = _dot(c_ref[...], a_ref[0]) + _dot(sn_ref[...], b_ref[0])
    o_ref[0] = (acc * scale).astype(BF16)


def _time_dft(a, bm, n, row_off, out_rows):
    nb, s, w = a.shape
    idx = lax.broadcasted_iota(jnp.int32, (n, n), 0) * lax.broadcasted_iota(jnp.int32, (n, n), 1)
    ang = (idx % n).astype(F32) * (2.0 * math.pi / n)
    cmat = jnp.cos(ang).astype(BF16)
    snmat = (-jnp.sin(ang)).astype(BF16)
    tm = min(n, TOKEN_TILE)
    scale = 1.0 / math.sqrt(n * FNET_GROUP_DIM)
    blk = row_off // n
    return pl.pallas_call(
        functools.partial(_dft_kernel, scale),
        grid=(nb, n // tm),
        in_specs=[pl.BlockSpec((tm, n), lambda b, i: (i, 0)),
                  pl.BlockSpec((tm, n), lambda b, i: (i, 0)),
                  pl.BlockSpec((1, n, w), lambda b, i: (b, blk, 0)),
                  pl.BlockSpec((1, n, w), lambda b, i: (b, blk, 0))],
        out_specs=pl.BlockSpec((1, tm, w), lambda b, i: (b, i, 0)),
        out_shape=jax.ShapeDtypeStruct((nb, out_rows, w), BF16),
        name=f"time_dft_{n}",
    )(cmat, snmat, a, bm)


def _pair_constants(lg_ref, p):
    c = RET_CHUNK
    lgf0, lgf1 = lg_ref[0, 2 * p], lg_ref[0, 2 * p + 1]
    lgb0, lgb1 = lg_ref[1, 2 * p], lg_ref[1, 2 * p + 1]
    lane = lax.broadcasted_iota(jnp.int32, (c, c), 1)
    rowi = lax.broadcasted_iota(jnp.int32, (c, c), 0)
    lo = lane < RET_DK
    rlo = rowi < RET_DK
    lgf_l = jnp.where(lo, lgf0, lgf1)
    lgb_l = jnp.where(lo, lgb0, lgb1)
    rf = rowi.astype(F32)
    i2 = lax.broadcasted_iota(jnp.int32, (c, 2 * c), 0)
    c2 = lax.broadcasted_iota(jnp.int32, (c, 2 * c), 1)
    second = c2 >= c
    diff = (i2 - jnp.where(second, c2 - c, c2)).astype(F32)
    return dict(
        wqf=jnp.exp((rf + 1.0) * lgf_l), wqb=jnp.exp((c - rf) * lgb_l),
        wkf=jnp.exp((c - 1.0 - rf) * lgf_l), wkb=jnp.exp(rf * lgb_l),
        gf=jnp.exp(c * jnp.where(rlo, lgf0, lgf1)), gb=jnp.exp(c * jnp.where(rlo, lgb0, lgb1)),
        dcat=jnp.where(diff > 0, jnp.exp(diff * jnp.where(second, lgf1, lgf0)),
                       jnp.where(diff < 0, jnp.exp(-diff * jnp.where(second, lgb1, lgb0)), 2.0)))


def _ret_kernel(n_xc, n_cc, n_pairs, lg_ref, q_ref, k_ref, v_ref, g_ref, nw_ref, o_ref, sb_ref):
    c = RET_CHUNK
    pairs = range(n_pairs)
    consts = [_pair_constants(lg_ref, pl.program_id(1) * n_pairs + pp) for pp in pairs]
    lane = lax.broadcasted_iota(jnp.int32, (c, c), 1)
    rowi = lax.broadcasted_iota(jnp.int32, (c, c), 0)
    lo = lane < RET_DK
    bmask = (rowi < RET_DK) == lo

    def load(ref, off, pp):
        return ref[0, pl.ds(off, c), pp * LANES:(pp + 1) * LANES].astype(F32)

    def split_heads(u):
        return jnp.concatenate([jnp.where(lo, u, 0.0), jnp.where(lo, 0.0, u)], axis=0).astype(BF16)

    def kv_update(states, fwd, off):
        out = []
        for pp in pairs:
            cst = consts[pp]
            kw = (load(k_ref, off, pp) * (cst["wkf"] if fwd else cst["wkb"])).astype(BF16)
            kv = lax.dot_general(kw, load(v_ref, off, pp).astype(BF16), (((0,), (0,)), ((), ())),
                                 preferred_element_type=F32)
            out.append((cst["gf"] if fwd else cst["gb"]) * states[pp] + jnp.where(bmask, kv, 0.0))
        return tuple(out)

    def emit(states, off, cidx):
        for pp in pairs:
            cst = consts[pp]
            q = load(q_ref, off, pp)
            scores = lax.dot_general(q.astype(BF16), split_heads(load(k_ref, off, pp)),
                                     (((1,), (1,)), ((), ())), preferred_element_type=F32)
            lhs = jnp.concatenate([(scores * cst["dcat"]).astype(BF16), (q * cst["wqf"]).astype(BF16),
                                   (q * cst["wqb"]).astype(BF16)], axis=1)
            rhs = jnp.concatenate([split_heads(load(v_ref, off, pp)), states[pp].astype(BF16),
                                   sb_ref[cidx, pp].astype(BF16)], axis=0)
            o = _dot(lhs, rhs)
            inv = 1.0 / RET_DK
            s0 = jnp.sum(jnp.where(lo, o, 0.0), axis=-1, keepdims=True)
            s1 = jnp.sum(o, axis=-1, keepdims=True) - s0
            dlt = o - jnp.where(lo, s0, s1) * inv
            d2 = dlt * dlt
            v0 = jnp.sum(jnp.where(lo, d2, 0.0), axis=-1, keepdims=True)
            v1 = jnp.sum(d2, axis=-1, keepdims=True) - v0
            on = dlt * lax.rsqrt(jnp.where(lo, v0, v1) * inv + NORM_EPS)
            y = _silu(load(g_ref, off, pp)) * on * nw_ref[:, pp * LANES:(pp + 1) * LANES]
            o_ref[0, pl.ds(off, c), pp * LANES:(pp + 1) * LANES] = y.astype(BF16)

    def save_bwd(cidx, states):
        for pp in pairs:
            sb_ref[cidx, pp] = states[pp]

    zero = tuple(jnp.zeros((c, c), F32) for _ in pairs)
    s = zero
    for cc in reversed(range(n_cc)):
        save_bwd(n_xc + cc, s)
        s = kv_update(s, False, (n_xc + cc) * c)

    def bwd_body(i, s):
        n = n_xc - 1 - i
        save_bwd(n, s)
        return kv_update(s, False, pl.multiple_of(n * c, c))

    lax.fori_loop(0, n_xc, bwd_body, s)

    s = zero
    for cc in range(n_cc):
        off = (n_xc + cc) * c
        emit(s, off, n_xc + cc)
        s = kv_update(s, True, off)

    def fwd_body(n, s):
        off = pl.multiple_of(n * c, c)
        emit(s, off, n)
        return kv_update(s, True, off)

    lax.fori_loop(0, n_xc, fwd_body, s)


RET_PAIRS_PER_STEP = 2


def _retention(q, k, v, g, log_g, ret_norm_w, n_lat):
    nb, s, w = q.shape
    n_xc = n_lat // RET_CHUNK
    n_cc = (s - n_lat) // RET_CHUNK
    n_pairs = RET_PAIRS_PER_STEP
    width = n_pairs * LANES
    seq = pl.BlockSpec((1, s, width), lambda b, p: (b, 0, p))
    return pl.pallas_call(
        functools.partial(_ret_kernel, n_xc, n_cc, n_pairs),
        grid=(nb, w // width),
        in_specs=[pl.BlockSpec(memory_space=pltpu.SMEM), seq, seq, seq, seq,
                  pl.BlockSpec((1, width), lambda b, p: (0, p))],
        out_specs=seq,
        out_shape=jax.ShapeDtypeStruct((nb, s, w), BF16),
        scratch_shapes=[pltpu.VMEM((n_xc + n_cc, n_pairs, RET_CHUNK, RET_CHUNK), F32)],
        name="retention",
    )(log_g, q, k, v, g, ret_norm_w)


def _postmix_kernel(nx_tiles, nb, yf_ref, yr_ref, x_ref, mods_ref, nw_ref, wout_ref, rw_ref, rb_ref,
                    xo_ref, h_ref, rec_ref, cnt_ref, carry_ref):
    b = pl.program_id(0)
    t = pl.program_id(1)

    @pl.when((b == 0) & (t == 0))
    def _():
        carry_ref[...] = jnp.zeros_like(carry_ref)

    row = jnp.where(t >= nx_tiles, nb, b)
    cat = jnp.concatenate([yf_ref[0], yr_ref[0]], axis=1)
    xn = x_ref[0] + _mod_row(mods_ref, row, 2) * _dot(cat, wout_ref[...])
    xo_ref[0] = xn
    h = _rmsnorm(xn, nw_ref[...]) * (1.0 + _mod_row(mods_ref, row, 4)) + _mod_row(mods_ref, row, 3)
    _slab_write(h_ref, h, lead=(0,))

    logits = _dot3(h, rw_ref[...]) + rb_ref[...]
    tile = logits.shape[0]
    lane = lax.broadcasted_iota(jnp.int32, logits.shape, 1)
    work = logits
    sels, vals = [], []
    for _ in range(TOP_K):
        m = jnp.max(work, axis=-1, keepdims=True)
        idx = jnp.min(jnp.where(work == m, lane, LANES), axis=-1, keepdims=True)
        sel = lane == idx
        sels.append(sel)
        vals.append(m)
        work = jnp.where(sel, -jnp.inf, work)
    exps = [jnp.exp(v - vals[0]) for v in vals]
    den = exps[0] + exps[1] + exps[2] + exps[3]

    onehot = jnp.zeros(logits.shape, F32)
    for sel in sels:
        onehot = onehot + sel.astype(F32)
    ri = lax.broadcasted_iota(jnp.int32, (tile, tile), 0)
    ci = lax.broadcasted_iota(jnp.int32, (tile, tile), 1)
    tri = (ri > ci).astype(BF16)
    base = _dot(tri, onehot.astype(BF16)) + carry_ref[...]
    lane_f = lane.astype(F32)
    rec = jnp.zeros(logits.shape, F32)
    for kk in range(TOP_K):
        e_idx = jnp.sum(jnp.where(sels[kk], lane_f, 0.0), axis=-1, keepdims=True)
        rank = jnp.sum(jnp.where(sels[kk], base, 0.0), axis=-1, keepdims=True)
        rec = jnp.where(lane == kk, e_idx, rec)
        rec = jnp.where(lane == TOP_K + kk, exps[kk] / den, rec)
        rec = jnp.where(lane == 2 * TOP_K + kk, rank, rec)
    rec_ref[0] = rec
    carry_ref[...] = carry_ref[...] + jnp.sum(onehot, axis=0, keepdims=True)
    cnt_ref[...] = carry_ref[...]


def _postmix(yf, yr, xc, mods, norm_w, w_out, router_w, router_b, n_lat):
    nb, s, d = xc.shape
    nt = s // TOKEN_TILE
    tok = lambda w: pl.BlockSpec((1, TOKEN_TILE, w), lambda b, t: (b, t, 0))
    full = lambda a: pl.BlockSpec(a.shape, lambda b, t: (0,) * a.ndim)
    return pl.pallas_call(
        functools.partial(_postmix_kernel, n_lat // TOKEN_TILE, nb),
        grid=(nb, nt),
        in_specs=[tok(FNET_WIDTH), tok(RET_WIDTH), tok(d), full(mods), full(norm_w), full(w_out),
                  full(router_w), full(router_b)],
        out_specs=[tok(d), pl.BlockSpec((1, TOKEN_TILE * SLAB, LANES), lambda b, t: (b, t, 0)), tok(LANES),
                   pl.BlockSpec((1, LANES), lambda b, t: (0, 0))],
        out_shape=[jax.ShapeDtypeStruct((nb, s, d), F32), jax.ShapeDtypeStruct((nb, s * SLAB, LANES), F32),
                   jax.ShapeDtypeStruct((nb, s, LANES), F32), jax.ShapeDtypeStruct((1, LANES), F32)],
        scratch_shapes=[pltpu.VMEM((1, LANES), F32)],
        compiler_params=pltpu.CompilerParams(dimension_semantics=("arbitrary", "arbitrary")),
        name="postmix_router",
    )(yf, yr, xc, mods, norm_w, w_out, router_w, router_b)


GLU_GROUP = 2 * LANES


def _w1_prep_kernel(w_ref, o_ref):
    src = lax.broadcasted_iota(jnp.int32, (GLU_GROUP, GLU_GROUP), 0)
    dst = lax.broadcasted_iota(jnp.int32, (GLU_GROUP, GLU_GROUP), 1)
    perm = (src == jnp.where(dst < LANES, 2 * dst, 2 * (dst - LANES) + 1)).astype(BF16)
    for j in range(w_ref.shape[2] // GLU_GROUP):
        cols = slice(j * GLU_GROUP, (j + 1) * GLU_GROUP)
        o_ref[0, :, cols] = _dot(w_ref[0, :, cols].astype(BF16), perm).astype(BF16)


def _w1_prep(w1):
    ne, d, width = w1.shape
    tn = 2 * GLU_GROUP
    return pl.pallas_call(
        _w1_prep_kernel,
        grid=(ne, width // tn),
        in_specs=[pl.BlockSpec((1, d, tn), lambda e, j: (e, 0, j))],
        out_specs=pl.BlockSpec((1, d, tn), lambda e, j: (e, 0, j)),
        out_shape=jax.ShapeDtypeStruct((ne, d, width), BF16),
        name="w1_prep",
    )(w1)


def _expert_kernel(be_ref, nu_ref, tok_ref, tokn_ref, slotp_ref, slot_ref, h_hbm,
                   w1_ref, b1_ref, w2_ref, b2_ref, out_hbm, xbuf, obuf, gsem, ssem):
    del be_ref
    b = pl.program_id(0)
    nu = nu_ref[0]
    rows = xbuf.shape[0] // SLAB
    n_scratch = out_hbm.shape[0] // SLAB - rows
    d_hidden = w1_ref.shape[2]
    d_out = w2_ref.shape[2]

    def hbm_row(ref, idx):
        return ref.at[pl.ds(pl.multiple_of(idx * SLAB, SLAB), SLAB)]

    def gather_row(idx_ref, j):
        return pltpu.make_async_copy(hbm_row(h_hbm, idx_ref[0, 0, j]), xbuf.at[pl.ds(j * SLAB, SLAB)], gsem.at[0])

    def scatter_row(idx_ref, j):
        return pltpu.make_async_copy(obuf.at[pl.ds(j * SLAB, SLAB)], hbm_row(out_hbm, idx_ref[0, 0, j]), ssem.at[0])

    def rolled(make, idx_ref):
        def body(j, carry):
            make(idx_ref, j).start()
            return carry
        lax.fori_loop(0, rows, body, 0)

    def gather_wait():
        pltpu.make_async_copy(h_hbm.at[pl.ds(0, rows * SLAB)], xbuf, gsem.at[0]).wait()

    def scatter_wait():
        pltpu.make_async_copy(obuf, out_hbm.at[pl.ds(0, rows * SLAB)], ssem.at[0]).wait()

    @pl.when(b < nu)
    def _():
        @pl.when(b == 0)
        def _():
            obuf[...] = jnp.zeros(obuf.shape, F32)
            rolled(gather_row, tok_ref)

        gather_wait()
        x = _slab_read(xbuf, rows).astype(BF16)

        n1 = d_hidden // GLU_GROUP
        scatter_chunks = n1 // 2
        per_s = rows // scatter_chunks
        per_g = rows // (n1 - scatter_chunks)
        acts = []
        for j in range(n1):
            cols = slice(j * GLU_GROUP, (j + 1) * GLU_GROUP)
            hb = _dot(x, w1_ref[0, :, cols]) + b1_ref[0, :, cols]
            x_glu = jnp.minimum(hb[:, :LANES], SWIGLU_LIMIT)
            x_lin = jnp.clip(hb[:, LANES:], -SWIGLU_LIMIT, SWIGLU_LIMIT)
            acts.append((x_glu * (1.0 / (1.0 + jnp.exp(-SWIGLU_ALPHA * x_glu))) * (x_lin + 1.0)).astype(BF16))
            if j < scatter_chunks:
                for r in range(j * per_s, (j + 1) * per_s):
                    scatter_row(slotp_ref, r).start(priority=r % 2)
            else:
                for r in range((j - scatter_chunks) * per_g, (j - scatter_chunks + 1) * per_g):
                    gather_row(tokn_ref, r).start(priority=r % 2)
        act = jnp.concatenate(acts, axis=1)
        scatter_wait()
        for j in range(d_out // GLU_GROUP):
            cols = slice(j * GLU_GROUP, (j + 1) * GLU_GROUP)
            _slab_write(obuf, _dot(act, w2_ref[0, :, cols]) + b2_ref[0, :, cols], col0=j * GLU_GROUP)

        @pl.when(b == nu - 1)
        def _():
            rolled(scatter_row, slot_ref)
            scatter_wait()
            gather_wait()
            obuf[...] = jnp.zeros(obuf.shape, F32)
            fill = pltpu.make_async_copy(obuf, out_hbm.at[pl.ds(n_scratch * SLAB, rows * SLAB)], ssem.at[0])
            fill.start()
            fill.wait()


def _experts(h_slabs, block_e, n_used, row_tok, row_slot, w1, b1, w2, b2, n_out_rows):
    n_blocks = block_e.shape[0]
    rows = EXPERT_ROWS
    smem_blk = lambda fn: pl.BlockSpec((1, 1, rows), fn, memory_space=pltpu.SMEM)
    wspec = lambda a: pl.BlockSpec((1,) + a.shape[1:], lambda i, be, nu: (be[i], 0, 0))
    grid_spec = pltpu.PrefetchScalarGridSpec(
        num_scalar_prefetch=2,
        grid=(n_blocks,),
        in_specs=[smem_blk(lambda i, be, nu: (i, 0, 0)),
                  smem_blk(lambda i, be, nu: (jnp.minimum(i + 1, n_blocks - 1), 0, 0)),
                  smem_blk(lambda i, be, nu: (i, 0, 0)),
                  smem_blk(lambda i, be, nu: (i + 1, 0, 0)),
                  pl.BlockSpec(memory_space=pl.ANY),
                  wspec(w1), wspec(b1), wspec(w2), wspec(b2)],
        out_specs=pl.BlockSpec(memory_space=pl.ANY),
        scratch_shapes=[pltpu.VMEM((rows * SLAB, LANES), F32), pltpu.VMEM((rows * SLAB, LANES), F32),
                        pltpu.SemaphoreType.DMA((1,)), pltpu.SemaphoreType.DMA((1,))],
    )
    return pl.pallas_call(
        _expert_kernel,
        grid_spec=grid_spec,
        out_shape=jax.ShapeDtypeStruct((n_out_rows * SLAB, LANES), F32),
        compiler_params=pltpu.CompilerParams(dimension_semantics=("arbitrary",)),
        name="expert_ffn",
    )(block_e, n_used, row_tok, row_tok, row_slot, row_slot, h_slabs, w1, b1, w2, b2)


def _combine_kernel(nx_tiles, nb, final, x_ref, o0_ref, o1_ref, o2_ref, o3_ref, rec_ref, mods_ref, fnw_ref,
                    out_ref):
    b = pl.program_id(0)
    t = pl.program_id(1)
    row = jnp.where(t >= nx_tiles, nb, b)
    rec = rec_ref[0]
    tile = rec.shape[0]
    y = rec[:, TOP_K:TOP_K + 1] * _slab_read(o0_ref, tile)
    for kk, o_ref in ((1, o1_ref), (2, o2_ref), (3, o3_ref)):
        y = y + rec[:, TOP_K + kk:TOP_K + kk + 1] * _slab_read(o_ref, tile)
    xn = x_ref[0] + _mod_row(mods_ref, row, 5) * y
    if final:
        xn = _rmsnorm(xn, fnw_ref[...])
    out_ref[0] = xn


def _combine(xn, expert_out, rec, mods, final_norm_w, n_lat, final):
    nb, s, d = xn.shape
    out_rows = n_lat if final else s
    nt = out_rows // TOKEN_TILE
    tiles_per_seq = s // TOKEN_TILE
    tiles_per_choice = nb * tiles_per_seq
    tok = lambda w: pl.BlockSpec((1, TOKEN_TILE, w), lambda b, t: (b, t, 0))
    full = lambda a: pl.BlockSpec(a.shape, lambda b, t: (0,) * a.ndim)
    choice = lambda kk: pl.BlockSpec((TOKEN_TILE * SLAB, LANES),
                                     lambda b, t: (kk * tiles_per_choice + b * tiles_per_seq + t, 0))
    return pl.pallas_call(
        functools.partial(_combine_kernel, n_lat // TOKEN_TILE, nb, final),
        grid=(nb, nt),
        in_specs=[tok(d)] + [choice(kk) for kk in range(TOP_K)] + [tok(LANES), full(mods), full(final_norm_w)],
        out_specs=tok(d),
        out_shape=jax.ShapeDtypeStruct((nb, out_rows, d), F32),
        name="combine_final" if final else "combine",
    )(xn, *([expert_out] * TOP_K), rec, mods, final_norm_w)


def _rope_tables(n_lat, n_ctx):
    pos = jnp.arange(n_lat)
    rowp = (pos // GRID_W).astype(F32)
    colp = (pos % GRID_W).astype(F32)
    n_freq = RET_DK // 4
    freqs = ROPE_BASE ** (-jnp.arange(n_freq, dtype=F32) / n_freq)
    ang = jnp.concatenate([rowp[:, None] * freqs, colp[:, None] * freqs], axis=-1)
    cos = jnp.tile(jnp.cos(ang), (1, LANES // (RET_DK // 2)))
    sin = jnp.tile(jnp.concatenate([-jnp.sin(ang), jnp.sin(ang)], axis=-1), (1, LANES // RET_DK))
    cos = jnp.concatenate([cos, jnp.ones((n_ctx, LANES), F32)], axis=0)
    sin = jnp.concatenate([sin, jnp.zeros((n_ctx, LANES), F32)], axis=0)
    return cos, sin


def _channel_dft_matrix():
    n = FNET_GROUP_DIM
    groups = FNET_WIDTH // n
    idx = jnp.arange(n)[:, None] * jnp.arange(n)[None, :]
    ang = (idx % n).astype(F32) * (2.0 * math.pi / n)
    eye = jnp.eye(groups, dtype=F32)
    return jnp.concatenate([jnp.kron(eye, jnp.cos(ang)), jnp.kron(eye, jnp.sin(ang))], axis=1).astype(BF16)


def _route_plan(rec, cnt, n_tok):
    rows = EXPERT_ROWS
    n_assign = n_tok * TOP_K
    n_blocks = n_assign // rows + N_EXPERTS
    rec2 = rec.reshape(n_tok, LANES)
    e_idx = rec2[:, 0:TOP_K].astype(jnp.int32)
    rank = rec2[:, 2 * TOP_K:3 * TOP_K].astype(jnp.int32)
    sizes = cnt[0, :N_EXPERTS].astype(jnp.int32)
    padded = (sizes + rows - 1) // rows * rows
    pend = jnp.cumsum(padded)
    pstart = pend - padded
    dest = (pstart[e_idx] + rank).reshape(-1)
    n_used = (pend[-1] // rows).astype(jnp.int32)
    blk = jnp.arange(n_blocks, dtype=jnp.int32)
    block_e = jnp.sum(((blk * rows)[:, None] >= pend[None, :]).astype(jnp.int32), axis=1)
    block_e = jnp.minimum(block_e, N_EXPERTS - 1)
    block_e = jnp.where(blk < n_used, block_e, block_e[jnp.maximum(n_used - 1, 0)])
    tok_ids = jnp.arange(n_tok, dtype=jnp.int32)[:, None]
    slot_of = (jnp.arange(TOP_K, dtype=jnp.int32)[None, :] * n_tok + tok_ids).reshape(-1)
    lead = n_assign + jnp.arange(rows, dtype=jnp.int32)
    pad_slot = jnp.tile(lead, n_blocks)
    row_slot = pad_slot.at[dest].set(slot_of, unique_indices=True, mode="promise_in_bounds")
    row_tok = jnp.where(row_slot < n_assign, row_slot % n_tok, 0)
    row_slot = jnp.concatenate([lead, row_slot])
    return (block_e, n_used.reshape(1), row_tok.reshape(n_blocks, 1, rows),
            row_slot.reshape(n_blocks + 1, 1, rows), n_assign + rows)


def kernel(x, c, ctx, c_ctx, mod_w, mod_b, norm_w, w_in, w_out, ret_decay, ret_norm_w, router_w, router_b,
           expert_w1, expert_b1, expert_w2, expert_b2, final_norm_w):
    nb, n_lat, d = x.shape
    n_ctx = ctx.shape[1]
    depth = mod_w.shape[0]
    s = n_lat + n_ctx
    n_tok = nb * s

    mod_rows = -(-(nb + 1) // MOD_ROWS_PAD) * MOD_ROWS_PAD
    cc = jnp.concatenate([c, c_ctx[None, :], jnp.zeros((mod_rows - nb - 1, d), F32)], axis=0)
    mods_all = _modulation(cc, mod_w, mod_b)

    cos_t, sin_t = _rope_tables(n_lat, n_ctx)
    cs = _channel_dft_matrix()
    log_g = jnp.log1p(-jnp.exp2(-ret_decay.astype(F32)))
    rw_pad = jnp.pad(router_w, ((0, 0), (0, 0), (0, LANES - N_EXPERTS)))
    rb_pad = jnp.pad(router_b, ((0, 0), (0, LANES - N_EXPERTS)), constant_values=-1e30)

    xc = jnp.concatenate([x, ctx], axis=1)
    out = None
    for l in range(depth):
        final = l == depth - 1
        mods = mods_all[l]
        a, bm, q, k, v, g = _premix(xc, mods, norm_w[l, 0][None, :], w_in[l].astype(BF16), cs,
                                    cos_t, sin_t, n_lat)
        yf_x = _time_dft(a, bm, n_lat, 0, n_lat)
        yf_c = _time_dft(a, bm, n_ctx, n_lat, n_ctx)
        yf = jnp.concatenate([yf_x, yf_c], axis=1)
        yr = _retention(q, k, v, g, log_g[l], ret_norm_w[l][None, :], n_lat)
        xn, h, rec, cnt = _postmix(yf, yr, xc, mods, norm_w[l, 1][None, :], w_out[l].astype(BF16),
                                   rw_pad[l], rb_pad[l][None, :], n_lat)

        block_e, n_used, row_tok, row_slot, n_out_rows = _route_plan(rec, cnt, n_tok)
        b1 = expert_b1[l]
        b1 = b1.reshape(b1.shape[0], -1, LANES, 2).transpose(0, 1, 3, 2).reshape(b1.shape[0], 1, -1)
        expert_out = _experts(h.reshape(n_tok * SLAB, LANES), block_e, n_used, row_tok, row_slot,
                              _w1_prep(expert_w1[l]), b1,
                              expert_w2[l].astype(BF16), expert_b2[l][:, None, :], n_out_rows)
        res = _combine(xn, expert_out, rec, mods, final_norm_w[None, :], n_lat, final)
        if final:
            out = res
        else:
            xc = res
    return out
```

```python
import functools
import math

import jax
import jax.numpy as jnp
from jax import lax
from jax.experimental import pallas as pl
from jax.experimental.pallas import tpu as pltpu

D_MODEL = 1024
N_MOD = 6
GRID_W = 64
FNET_GROUP_DIM = 64
FNET_WIDTH = 512
RET_HEADS = 8
RET_DK = 64
RET_WIDTH = 512
RET_CHUNK = 128
IN_WIDTH = 2560
ROPE_BASE = 10000.0
N_EXPERTS = 32
TOP_K = 4
SWIGLU_LIMIT = 7.0
SWIGLU_ALPHA = 1.702
NORM_EPS = 1e-6

LANES = 128
TOKEN_TILE = 256
EXPERT_ROWS = 512
MOD_ROWS_PAD = 8

F32 = jnp.float32
BF16 = jnp.bfloat16


def _dot(a, b):
    return jnp.dot(a, b, preferred_element_type=F32)


def _split_bf16(a):
    hi = a.astype(BF16)
    lo = (a - hi.astype(F32)).astype(BF16)
    return hi, lo


def _dot3(a, b):
    a_hi, a_lo = _split_bf16(a)
    b_hi, b_lo = _split_bf16(b)
    return _dot(a_hi, b_hi) + _dot(a_lo, b_hi) + _dot(a_hi, b_lo)


def _silu(a):
    return a * (1.0 / (1.0 + jnp.exp(-a)))


def _rmsnorm(x, w):
    ms = jnp.mean(x * x, axis=-1, keepdims=True)
    return x * lax.rsqrt(ms + NORM_EPS) * w


SLAB = D_MODEL // LANES


def _slab_read(ref, rows, lead=()):
    return jnp.concatenate([ref[lead + (pl.ds(s, rows, stride=SLAB), slice(None))] for s in range(SLAB)], axis=1)


def _slab_write(ref, val, col0=0, lead=()):
    rows = val.shape[0]
    for u in range(val.shape[1] // LANES):
        ref[lead + (pl.ds(col0 // LANES + u, rows, stride=SLAB), slice(None))] = val[:, u * LANES:(u + 1) * LANES]


def _mod_kernel(cc_ref, w_ref, b_ref, o_ref):
    o_ref[0] = _dot3(_silu(cc_ref[...]), w_ref[0]) + b_ref[0]


def _modulation(cc, mod_w, mod_b):
    depth, d, width = mod_w.shape
    rows = cc.shape[0]
    tn = width // 4
    return pl.pallas_call(
        _mod_kernel,
        grid=(depth, width // tn),
        in_specs=[
            pl.BlockSpec((rows, d), lambda l, j: (0, 0)),
            pl.BlockSpec((1, d, tn), lambda l, j: (l, 0, j)),
            pl.BlockSpec((1, 1, tn), lambda l, j: (l, 0, j)),
        ],
        out_specs=pl.BlockSpec((1, rows, tn), lambda l, j: (l, 0, j)),
        out_shape=jax.ShapeDtypeStruct((depth, rows, width), F32),
        name="modulation",
    )(cc, mod_w, mod_b.reshape(depth, 1, width))


def _mod_row(mods_ref, row, k):
    return mods_ref[pl.ds(row, 1), pl.ds(k * D_MODEL, D_MODEL)]


def _premix_kernel(nx_tiles, nb, x_ref, mods_ref, nw_ref, win_ref, cs_ref, cos_ref, sin_ref,
                   a_ref, b_ref, q_ref, k_ref, v_ref, g_ref):
    b = pl.program_id(0)
    t = pl.program_id(1)
    row = jnp.where(t >= nx_tiles, nb, b)
    y = _rmsnorm(x_ref[0], nw_ref[...])
    h = (y * (1.0 + _mod_row(mods_ref, row, 1)) + _mod_row(mods_ref, row, 0)).astype(BF16)
    z = _dot(h, win_ref[...])
    f = z[:, :FNET_WIDTH].astype(BF16)
    ab = _dot(f, cs_ref[...])
    a_ref[0] = ab[:, :FNET_WIDTH].astype(BF16)
    b_ref[0] = ab[:, FNET_WIDTH:].astype(BF16)

    cos = cos_ref[...]
    sin = sin_ref[...]
    lane = lax.broadcasted_iota(jnp.int32, cos.shape, 1)
    first_half = (lane % RET_DK) < (RET_DK // 2)

    def rope(u):
        outs = []
        for j in range(RET_WIDTH // LANES):
            c = u[:, j * LANES:(j + 1) * LANES]
            partner = jnp.where(first_half,
                                pltpu.roll(c, LANES - RET_DK // 2, 1),
                                pltpu.roll(c, RET_DK // 2, 1))
            outs.append(c * cos + partner * sin)
        return jnp.concatenate(outs, axis=1)

    o = FNET_WIDTH
    q_ref[0] = rope(z[:, o:o + RET_WIDTH] * (RET_DK ** -0.5)).astype(BF16)
    k_ref[0] = rope(z[:, o + RET_WIDTH:o + 2 * RET_WIDTH]).astype(BF16)
    v_ref[0] = z[:, o + 2 * RET_WIDTH:o + 3 * RET_WIDTH].astype(BF16)
    g_ref[0] = z[:, o + 3 * RET_WIDTH:o + 4 * RET_WIDTH].astype(BF16)


def _premix(xc, mods, norm_w, w_in, cs, cos_t, sin_t, n_lat):
    nb, s, d = xc.shape
    nt = s // TOKEN_TILE
    tok = lambda w: pl.BlockSpec((1, TOKEN_TILE, w), lambda b, t: (b, t, 0))
    full = lambda a: pl.BlockSpec(a.shape, lambda b, t: (0,) * a.ndim)
    out = jax.ShapeDtypeStruct((nb, s, RET_WIDTH), BF16)
    return pl.pallas_call(
        functools.partial(_premix_kernel, n_lat // TOKEN_TILE, nb),
        grid=(nb, nt),
        in_specs=[tok(d), full(mods), full(norm_w), full(w_in), full(cs),
                  pl.BlockSpec((TOKEN_TILE, LANES), lambda b, t: (t, 0)),
                  pl.BlockSpec((TOKEN_TILE, LANES), lambda b, t: (t, 0))],
        out_specs=[tok(RET_WIDTH)] * 6,
        out_shape=[out] * 6,
        name="premix",
    )(xc, mods, norm_w, w_in, cs, cos_t, sin_t)


def _dft_kernel(scale, c_ref, sn_ref, a_ref, b_ref, o_ref):
    acc = _dot(c_ref[...], a_ref[0]) + _dot(sn_ref[...], b_ref[0])
    o_ref[0] = (acc * scale).astype(BF16)


def _time_dft(a, bm, n, row_off, out_rows):
    nb, s, w = a.shape
    idx = lax.broadcasted_iota(jnp.int32, (n, n), 0) * lax.broadcasted_iota(jnp.int32, (n, n), 1)
    ang = (idx % n).astype(F32) * (2.0 * math.pi / n)
    cmat = jnp.cos(ang).astype(BF16)
    snmat = (-jnp.sin(ang)).astype(BF16)
    tm = min(n, TOKEN_TILE)
    scale = 1.0 / math.sqrt(n * FNET_GROUP_DIM)
    blk = row_off // n
    return pl.pallas_call(
        functools.partial(_dft_kernel, scale),
        grid=(nb, n // tm),
        in_specs=[pl.BlockSpec((tm, n), lambda b, i: (i, 0)),
                  pl.BlockSpec((tm, n), lambda b, i: (i, 0)),
                  pl.BlockSpec((1, n, w), lambda b, i: (b, blk, 0)),
                  pl.BlockSpec((1, n, w), lambda b, i: (b, blk, 0))],
        out_specs=pl.BlockSpec((1, tm, w), lambda b, i: (b, i, 0)),
        out_shape=jax.ShapeDtypeStruct((nb, out_rows, w), BF16),
        name=f"time_dft_{n}",
    )(cmat, snmat, a, bm)


def _pair_constants(lg_ref, p):
    c = RET_CHUNK
    lgf0, lgf1 = lg_ref[0, 2 * p], lg_ref[0, 2 * p + 1]
    lgb0, lgb1 = lg_ref[1, 2 * p], lg_ref[1, 2 * p + 1]
    lane = lax.broadcasted_iota(jnp.int32, (c, c), 1)
    rowi = lax.broadcasted_iota(jnp.int32, (c, c), 0)
    lo = lane < RET_DK
    rlo = rowi < RET_DK
    lgf_l = jnp.where(lo, lgf0, lgf1)
    lgb_l = jnp.where(lo, lgb0, lgb1)
    rf = rowi.astype(F32)
    i2 = lax.broadcasted_iota(jnp.int32, (c, 2 * c), 0)
    c2 = lax.broadcasted_iota(jnp.int32, (c, 2 * c), 1)
    second = c2 >= c
    diff = (i2 - jnp.where(second, c2 - c, c2)).astype(F32)
    return dict(
        wqf=jnp.exp((rf + 1.0) * lgf_l), wqb=jnp.exp((c - rf) * lgb_l),
        wkf=jnp.exp((c - 1.0 - rf) * lgf_l), wkb=jnp.exp(rf * lgb_l),
        gf=jnp.exp(c * jnp.where(rlo, lgf0, lgf1)), gb=jnp.exp(c * jnp.where(rlo, lgb0, lgb1)),
        dcat=jnp.where(diff > 0, jnp.exp(diff * jnp.where(second, lgf1, lgf0)),
                       jnp.where(diff < 0, jnp.exp(-diff * jnp.where(second, lgb1, lgb0)), 2.0)))


def _ret_kernel(n_xc, n_cc, n_pairs, lg_ref, q_ref, k_ref, v_ref, g_ref, nw_ref, o_ref, sb_ref):
    c = RET_CHUNK
    pairs = range(n_pairs)
    consts = [_pair_constants(lg_ref, pl.program_id(1) * n_pairs + pp) for pp in pairs]
    lane = lax.broadcasted_iota(jnp.int32, (c, c), 1)
    rowi = lax.broadcasted_iota(jnp.int32, (c, c), 0)
    lo = lane < RET_DK
    bmask = (rowi < RET_DK) == lo

    def load(ref, off, pp):
        return ref[0, pl.ds(off, c), pp * LANES:(pp + 1) * LANES].astype(F32)

    def split_heads(u):
        return jnp.concatenate([jnp.where(lo, u, 0.0), jnp.where(lo, 0.0, u)], axis=0).astype(BF16)

    def kv_update(states, fwd, off):
        out = []
        for pp in pairs:
            cst = consts[pp]
            kw = (load(k_ref, off, pp) * (cst["wkf"] if fwd else cst["wkb"])).astype(BF16)
            kv = lax.dot_general(kw, load(v_ref, off, pp).astype(BF16), (((0,), (0,)), ((), ())),
                                 preferred_element_type=F32)
            out.append((cst["gf"] if fwd else cst["gb"]) * states[pp] + jnp.where(bmask, kv, 0.0))
        return tuple(out)

    def emit(states, off, cidx):
        for pp in pairs:
            cst = consts[pp]
            q = load(q_ref, off, pp)
            scores = lax.dot_general(q.astype(BF16), split_heads(load(k_ref, off, pp)),
                                     (((1,), (1,)), ((), ())), preferred_element_type=F32)
            lhs = jnp.concatenate([(scores * cst["dcat"]).astype(BF16), (q * cst["wqf"]).astype(BF16),
                                   (q * cst["wqb"]).astype(BF16)], axis=1)
            rhs = jnp.concatenate([split_heads(load(v_ref, off, pp)), states[pp].astype(BF16),
                                   sb_ref[cidx, pp].astype(BF16)], axis=0)
            o = _dot(lhs, rhs)
            inv = 1.0 / RET_DK
            s0 = jnp.sum(jnp.where(lo, o, 0.0), axis=-1, keepdims=True)
            s1 = jnp.sum(o, axis=-1, keepdims=True) - s0
            dlt = o - jnp.where(lo, s0, s1) * inv
            d2 = dlt * dlt
            v0 = jnp.sum(jnp.where(lo, d2, 0.0), axis=-1, keepdims=True)
            v1 = jnp.sum(d2, axis=-1, keepdims=True) - v0
            on = dlt * lax.rsqrt(jnp.where(lo, v0, v1) * inv + NORM_EPS)
            y = _silu(load(g_ref, off, pp)) * on * nw_ref[:, pp * LANES:(pp + 1) * LANES]
            o_ref[0, pl.ds(off, c), pp * LANES:(pp + 1) * LANES] = y.astype(BF16)

    def save_bwd(cidx, states):
        for pp in pairs:
            sb_ref[cidx, pp] = states[pp]

    zero = tuple(jnp.zeros((c, c), F32) for _ in pairs)
    s = zero
    for cc in reversed(range(n_cc)):
        save_bwd(n_xc + cc, s)
        s = kv_update(s, False, (n_xc + cc) * c)

    def bwd_body(i, s):
        n = n_xc - 1 - i
        save_bwd(n, s)
        return kv_update(s, False, pl.multiple_of(n * c, c))

    lax.fori_loop(0, n_xc, bwd_body, s)

    s = zero
    for cc in range(n_cc):
        off = (n_xc + cc) * c
        emit(s, off, n_xc + cc)
        s = kv_update(s, True, off)

    def fwd_body(n, s):
        off = pl.multiple_of(n * c, c)
        emit(s, off, n)
        return kv_update(s, True, off)

    lax.fori_loop(0, n_xc, fwd_body, s)


RET_PAIRS_PER_STEP = 2


def _retention(q, k, v, g, log_g, ret_norm_w, n_lat):
    nb, s, w = q.shape
    n_xc = n_lat // RET_CHUNK
    n_cc = (s - n_lat) // RET_CHUNK
    n_pairs = RET_PAIRS_PER_STEP
    width = n_pairs * LANES
    seq = pl.BlockSpec((1, s, width), lambda b, p: (b, 0, p))
    return pl.pallas_call(
        functools.partial(_ret_kernel, n_xc, n_cc, n_pairs),
        grid=(nb, w // width),
        in_specs=[pl.BlockSpec(memory_space=pltpu.SMEM), seq, seq, seq, seq,
                  pl.BlockSpec((1, width), lambda b, p: (0, p))],
        out_specs=seq,
        out_shape=jax.ShapeDtypeStruct((nb, s, w), BF16),
        scratch_shapes=[pltpu.VMEM((n_xc + n_cc, n_pairs, RET_CHUNK, RET_CHUNK), F32)],
        name="retention",
    )(log_g, q, k, v, g, ret_norm_w)


def _postmix_kernel(nx_tiles, nb, yf_ref, yr_ref, x_ref, mods_ref, nw_ref, wout_ref, rw_ref, rb_ref,
                    xo_ref, h_ref, rec_ref, cnt_ref, carry_ref):
    b = pl.program_id(0)
    t = pl.program_id(1)

    @pl.when((b == 0) & (t == 0))
    def _():
        carry_ref[...] = jnp.zeros_like(carry_ref)

    row = jnp.where(t >= nx_tiles, nb, b)
    cat = jnp.concatenate([yf_ref[0], yr_ref[0]], axis=1)
    xn = x_ref[0] + _mod_row(mods_ref, row, 2) * _dot(cat, wout_ref[...])
    xo_ref[0] = xn
    h = _rmsnorm(xn, nw_ref[...]) * (1.0 + _mod_row(mods_ref, row, 4)) + _mod_row(mods_ref, row, 3)
    _slab_write(h_ref, h, lead=(0,))

    logits = _dot3(h, rw_ref[...]) + rb_ref[...]
    tile = logits.shape[0]
    lane = lax.broadcasted_iota(jnp.int32, logits.shape, 1)
    work = logits
    sels, vals = [], []
    for _ in range(TOP_K):
        m = jnp.max(work, axis=-1, keepdims=True)
        idx = jnp.min(jnp.where(work == m, lane, LANES), axis=-1, keepdims=True)
        sel = lane == idx
        sels.append(sel)
        vals.append(m)
        work = jnp.where(sel, -jnp.inf, work)
    exps = [jnp.exp(v - vals[0]) for v in vals]
    den = exps[0] + exps[1] + exps[2] + exps[3]

    onehot = jnp.zeros(logits.shape, F32)
    for sel in sels:
        onehot = onehot + sel.astype(F32)
    ri = lax.broadcasted_iota(jnp.int32, (tile, tile), 0)
    ci = lax.broadcasted_iota(jnp.int32, (tile, tile), 1)
    tri = (ri > ci).astype(BF16)
    base = _dot(tri, onehot.astype(BF16)) + carry_ref[...]
    lane_f = lane.astype(F32)
    rec = jnp.zeros(logits.shape, F32)
    for kk in range(TOP_K):
        e_idx = jnp.sum(jnp.where(sels[kk], lane_f, 0.0), axis=-1, keepdims=True)
        rank = jnp.sum(jnp.where(sels[kk], base, 0.0), axis=-1, keepdims=True)
        rec = jnp.where(lane == kk, e_idx, rec)
        rec = jnp.where(lane == TOP_K + kk, exps[kk] / den, rec)
        rec = jnp.where(lane == 2 * TOP_K + kk, rank, rec)
    rec_ref[0] = rec
    carry_ref[...] = carry_ref[...] + jnp.sum(onehot, axis=0, keepdims=True)
    cnt_ref[...] = carry_ref[...]


def _postmix(yf, yr, xc, mods, norm_w, w_out, router_w, router_b, n_lat):
    nb, s, d = xc.shape
    nt = s // TOKEN_TILE
    tok = lambda w: pl.BlockSpec((1, TOKEN_TILE, w), lambda b, t: (b, t, 0))
    full = lambda a: pl.BlockSpec(a.shape, lambda b, t: (0,) * a.ndim)
    return pl.pallas_call(
        functools.partial(_postmix_kernel, n_lat // TOKEN_TILE, nb),
        grid=(nb, nt),
        in_specs=[tok(FNET_WIDTH), tok(RET_WIDTH), tok(d), full(mods), full(norm_w), full(w_out),
                  full(router_w), full(router_b)],
        out_specs=[tok(d), pl.BlockSpec((1, TOKEN_TILE * SLAB, LANES), lambda b, t: (b, t, 0)), tok(LANES),
                   pl.BlockSpec((1, LANES), lambda b, t: (0, 0))],
        out_shape=[jax.ShapeDtypeStruct((nb, s, d), F32), jax.ShapeDtypeStruct((nb, s * SLAB, LANES), F32),
                   jax.ShapeDtypeStruct((nb, s, LANES), F32), jax.ShapeDtypeStruct((1, LANES), F32)],
        scratch_shapes=[pltpu.VMEM((1, LANES), F32)],
        compiler_params=pltpu.CompilerParams(dimension_semantics=("arbitrary", "arbitrary")),
        name="postmix_router",
    )(yf, yr, xc, mods, norm_w, w_out, router_w, router_b)


GLU_GROUP = 2 * LANES


def _w1_prep_kernel(w_ref, o_ref):
    src = lax.broadcasted_iota(jnp.int32, (GLU_GROUP, GLU_GROUP), 0)
    dst = lax.broadcasted_iota(jnp.int32, (GLU_GROUP, GLU_GROUP), 1)
    perm = (src == jnp.where(dst < LANES, 2 * dst, 2 * (dst - LANES) + 1)).astype(BF16)
    for j in range(w_ref.shape[2] // GLU_GROUP):
        cols = slice(j * GLU_GROUP, (j + 1) * GLU_GROUP)
        o_ref[0, :, cols] = _dot(w_ref[0, :, cols].astype(BF16), perm).astype(BF16)


def _w1_prep(w1):
    ne, d, width = w1.shape
    tn = 2 * GLU_GROUP
    return pl.pallas_call(
        _w1_prep_kernel,
        grid=(ne, width // tn),
        in_specs=[pl.BlockSpec((1, d, tn), lambda e, j: (e, 0, j))],
        out_specs=pl.BlockSpec((1, d, tn), lambda e, j: (e, 0, j)),
        out_shape=jax.ShapeDtypeStruct((ne, d, width), BF16),
        name="w1_prep",
    )(w1)


def _expert_kernel(be_ref, nu_ref, tok_ref, tokn_ref, slotp_ref, slot_ref, h_hbm,
                   w1_ref, b1_ref, w2_ref, b2_ref, out_hbm, xbuf, obuf, xb_ref, act_ref, gsem, ssem):
    del be_ref
    b = pl.program_id(0)
    nu = nu_ref[0]
    rows = xbuf.shape[0] // SLAB
    n_scratch = out_hbm.shape[0] // SLAB - rows
    d_hidden = w1_ref.shape[2]
    d_out = w2_ref.shape[2]

    def hbm_row(ref, idx):
        return ref.at[pl.ds(pl.multiple_of(idx * SLAB, SLAB), SLAB)]

    def gather_row(idx_ref, j):
        return pltpu.make_async_copy(hbm_row(h_hbm, idx_ref[0, 0, j]), xbuf.at[pl.ds(j * SLAB, SLAB)], gsem.at[0])

    def scatter_row(idx_ref, j):
        return pltpu.make_async_copy(obuf.at[pl.ds(j * SLAB, SLAB)], hbm_row(out_hbm, idx_ref[0, 0, j]), ssem.at[0])

    def rolled(make, idx_ref):
        def body(j, carry):
            make(idx_ref, j).start()
            return carry
        lax.fori_loop(0, rows, body, 0)

    def gather_wait():
        pltpu.make_async_copy(h_hbm.at[pl.ds(0, rows * SLAB)], xbuf, gsem.at[0]).wait()

    def scatter_wait():
        pltpu.make_async_copy(obuf, out_hbm.at[pl.ds(0, rows * SLAB)], ssem.at[0]).wait()

    @pl.when(b < nu)
    def _():
        @pl.when(b == 0)
        def _():
            obuf[...] = jnp.zeros(obuf.shape, F32)
            rolled(gather_row, tok_ref)

        gather_wait()
        xb_ref[...] = _slab_read(xbuf, rows).astype(BF16)
        n1 = d_hidden // GLU_GROUP

        def hidden_chunks(lo, hi):
            for j in range(lo, hi):
                cols = slice(j * GLU_GROUP, (j + 1) * GLU_GROUP)
                hb = _dot(xb_ref[...], w1_ref[0, :, cols]) + b1_ref[0, :, cols]
                x_glu = jnp.minimum(hb[:, :LANES], SWIGLU_LIMIT)
                x_lin = jnp.clip(hb[:, LANES:], -SWIGLU_LIMIT, SWIGLU_LIMIT)
                act = x_glu * (1.0 / (1.0 + jnp.exp(-SWIGLU_ALPHA * x_glu))) * (x_lin + 1.0)
                act_ref[:, j * LANES:(j + 1) * LANES] = act.astype(BF16)

        @pl.when(nu > 0)
        def _():
            hidden_chunks(0, n1 // 2)
            for r in range(rows):
                scatter_row(slotp_ref, r).start(priority=r % 2)

        @pl.when(b >= 0)
        def _():
            for r in range(rows):
                gather_row(tokn_ref, r).start(priority=r % 2)
            hidden_chunks(n1 // 2, n1)

        scatter_wait()
        act = act_ref[...]
        for j in range(d_out // GLU_GROUP):
            cols = slice(j * GLU_GROUP, (j + 1) * GLU_GROUP)
            _slab_write(obuf, _dot(act, w2_ref[0, :, cols]) + b2_ref[0, :, cols], col0=j * GLU_GROUP)

        @pl.when(b == nu - 1)
        def _():
            rolled(scatter_row, slot_ref)
            scatter_wait()
            gather_wait()
            obuf[...] = jnp.zeros(obuf.shape, F32)
            fill = pltpu.make_async_copy(obuf, out_hbm.at[pl.ds(n_scratch * SLAB, rows * SLAB)], ssem.at[0])
            fill.start()
            fill.wait()


def _experts(h_slabs, block_e, n_used, row_tok, row_slot, w1, b1, w2, b2, n_out_rows):
    n_blocks = block_e.shape[0]
    rows = EXPERT_ROWS
    smem_blk = lambda fn: pl.BlockSpec((1, 1, rows), fn, memory_space=pltpu.SMEM)
    wspec = lambda a: pl.BlockSpec((1,) + a.shape[1:], lambda i, be, nu: (be[i], 0, 0))
    grid_spec = pltpu.PrefetchScalarGridSpec(
        num_scalar_prefetch=2,
        grid=(n_blocks,),
        in_specs=[smem_blk(lambda i, be, nu: (i, 0, 0)),
                  smem_blk(lambda i, be, nu: (jnp.minimum(i + 1, n_blocks - 1), 0, 0)),
                  smem_blk(lambda i, be, nu: (i, 0, 0)),
                  smem_blk(lambda i, be, nu: (i + 1, 0, 0)),
                  pl.BlockSpec(memory_space=pl.ANY),
                  wspec(w1), wspec(b1), wspec(w2), wspec(b2)],
        out_specs=pl.BlockSpec(memory_space=pl.ANY),
        scratch_shapes=[pltpu.VMEM((rows * SLAB, LANES), F32), pltpu.VMEM((rows * SLAB, LANES), F32),
                        pltpu.VMEM((rows, w1.shape[1]), BF16), pltpu.VMEM((rows, w2.shape[1]), BF16),
                        pltpu.SemaphoreType.DMA((1,)), pltpu.SemaphoreType.DMA((1,))],
    )
    return pl.pallas_call(
        _expert_kernel,
        grid_spec=grid_spec,
        out_shape=jax.ShapeDtypeStruct((n_out_rows * SLAB, LANES), F32),
        compiler_params=pltpu.CompilerParams(dimension_semantics=("arbitrary",)),
        name="expert_ffn",
    )(block_e, n_used, row_tok, row_tok, row_slot, row_slot, h_slabs, w1, b1, w2, b2)


def _combine_kernel(nx_tiles, nb, final, x_ref, o0_ref, o1_ref, o2_ref, o3_ref, rec_ref, mods_ref, fnw_ref,
                    out_ref):
    b = pl.program_id(0)
    t = pl.program_id(1)
    row = jnp.where(t >= nx_tiles, nb, b)
    rec = rec_ref[0]
    tile = rec.shape[0]
    y = rec[:, TOP_K:TOP_K + 1] * _slab_read(o0_ref, tile)
    for kk, o_ref in ((1, o1_ref), (2, o2_ref), (3, o3_ref)):
        y = y + rec[:, TOP_K + kk:TOP_K + kk + 1] * _slab_read(o_ref, tile)
    xn = x_ref[0] + _mod_row(mods_ref, row, 5) * y
    if final:
        xn = _rmsnorm(xn, fnw_ref[...])
    out_ref[0] = xn


def _combine(xn, expert_out, rec, mods, final_norm_w, n_lat, final):
    nb, s, d = xn.shape
    out_rows = n_lat if final else s
    nt = out_rows // TOKEN_TILE
    tiles_per_seq = s // TOKEN_TILE
    tiles_per_choice = nb * tiles_per_seq
    tok = lambda w: pl.BlockSpec((1, TOKEN_TILE, w), lambda b, t: (b, t, 0))
    full = lambda a: pl.BlockSpec(a.shape, lambda b, t: (0,) * a.ndim)
    choice = lambda kk: pl.BlockSpec((TOKEN_TILE * SLAB, LANES),
                                     lambda b, t: (kk * tiles_per_choice + b * tiles_per_seq + t, 0))
    return pl.pallas_call(
        functools.partial(_combine_kernel, n_lat // TOKEN_TILE, nb, final),
        grid=(nb, nt),
        in_specs=[tok(d)] + [choice(kk) for kk in range(TOP_K)] + [tok(LANES), full(mods), full(final_norm_w)],
        out_specs=tok(d),
        out_shape=jax.ShapeDtypeStruct((nb, out_rows, d), F32),
        name="combine_final" if final else "combine",
    )(xn, *([expert_out] * TOP_K), rec, mods, final_norm_w)


def _rope_tables(n_lat, n_ctx):
    pos = jnp.arange(n_lat)
    rowp = (pos // GRID_W).astype(F32)
    colp = (pos % GRID_W).astype(F32)
    n_freq = RET_DK // 4
    freqs = ROPE_BASE ** (-jnp.arange(n_freq, dtype=F32) / n_freq)
    ang = jnp.concatenate([rowp[:, None] * freqs, colp[:, None] * freqs], axis=-1)
    cos = jnp.tile(jnp.cos(ang), (1, LANES // (RET_DK // 2)))
    sin = jnp.tile(jnp.concatenate([-jnp.sin(ang), jnp.sin(ang)], axis=-1), (1, LANES // RET_DK))
    cos = jnp.concatenate([cos, jnp.ones((n_ctx, LANES), F32)], axis=0)
    sin = jnp.concatenate([sin, jnp.zeros((n_ctx, LANES), F32)], axis=0)
    return cos, sin


def _channel_dft_matrix():
    n = FNET_GROUP_DIM
    groups = FNET_WIDTH // n
    idx = jnp.arange(n)[:, None] * jnp.arange(n)[None, :]
    ang = (idx % n).astype(F32) * (2.0 * math.pi / n)
    eye = jnp.eye(groups, dtype=F32)
    return jnp.concatenate([jnp.kron(eye, jnp.cos(ang)), jnp.kron(eye, jnp.sin(ang))], axis=1).astype(BF16)


def _route_plan(rec, cnt, n_tok):
    rows = EXPERT_ROWS
    n_assign = n_tok * TOP_K
    n_blocks = n_assign // rows + N_EXPERTS
    rec2 = rec.reshape(n_tok, LANES)
    e_idx = rec2[:, 0:TOP_K].astype(jnp.int32)
    rank = rec2[:, 2 * TOP_K:3 * TOP_K].astype(jnp.int32)
    sizes = cnt[0, :N_EXPERTS].astype(jnp.int32)
    padded = (sizes + rows - 1) // rows * rows
    pend = jnp.cumsum(padded)
    pstart = pend - padded
    dest = (pstart[e_idx] + rank).reshape(-1)
    n_used = (pend[-1] // rows).astype(jnp.int32)
    blk = jnp.arange(n_blocks, dtype=jnp.int32)
    block_e = jnp.sum(((blk * rows)[:, None] >= pend[None, :]).astype(jnp.int32), axis=1)
    block_e = jnp.minimum(block_e, N_EXPERTS - 1)
    block_e = jnp.where(blk < n_used, block_e, block_e[jnp.maximum(n_used - 1, 0)])
    tok_ids = jnp.arange(n_tok, dtype=jnp.int32)[:, None]
    slot_of = (jnp.arange(TOP_K, dtype=jnp.int32)[None, :] * n_tok + tok_ids).reshape(-1)
    lead = n_assign + jnp.arange(rows, dtype=jnp.int32)
    pad_slot = jnp.tile(lead, n_blocks)
    row_slot = pad_slot.at[dest].set(slot_of, unique_indices=True, mode="promise_in_bounds")
    row_tok = jnp.where(row_slot < n_assign, row_slot % n_tok, 0)
    row_slot = jnp.concatenate([lead, row_slot])
    return (block_e, n_used.reshape(1), row_tok.reshape(n_blocks, 1, rows),
            row_slot.reshape(n_blocks + 1, 1, rows), n_assign + rows)


def kernel(x, c, ctx, c_ctx, mod_w, mod_b, norm_w, w_in, w_out, ret_decay, ret_norm_w, router_w, router_b,
           expert_w1, expert_b1, expert_w2, expert_b2, final_norm_w):
    nb, n_lat, d = x.shape
    n_ctx = ctx.shape[1]
    depth = mod_w.shape[0]
    s = n_lat + n_ctx
    n_tok = nb * s

    mod_rows = -(-(nb + 1) // MOD_ROWS_PAD) * MOD_ROWS_PAD
    cc = jnp.concatenate([c, c_ctx[None, :], jnp.zeros((mod_rows - nb - 1, d), F32)], axis=0)
    mods_all = _modulation(cc, mod_w, mod_b)

    cos_t, sin_t = _rope_tables(n_lat, n_ctx)
    cs = _channel_dft_matrix()
    log_g = jnp.log1p(-jnp.exp2(-ret_decay.astype(F32)))
    rw_pad = jnp.pad(router_w, ((0, 0), (0, 0), (0, LANES - N_EXPERTS)))
    rb_pad = jnp.pad(router_b, ((0, 0), (0, LANES - N_EXPERTS)), constant_values=-1e30)

    xc = jnp.concatenate([x, ctx], axis=1)
    out = None
    for l in range(depth):
        final = l == depth - 1
        mods = mods_all[l]
        a, bm, q, k, v, g = _premix(xc, mods, norm_w[l, 0][None, :], w_in[l].astype(BF16), cs,
                                    cos_t, sin_t, n_lat)
        yf_x = _time_dft(a, bm, n_lat, 0, n_lat)
        yf_c = _time_dft(a, bm, n_ctx, n_lat, n_ctx)
        yf = jnp.concatenate([yf_x, yf_c], axis=1)
        yr = _retention(q, k, v, g, log_g[l], ret_norm_w[l][None, :], n_lat)
        xn, h, rec, cnt = _postmix(yf, yr, xc, mods, norm_w[l, 1][None, :], w_out[l].astype(BF16),
                                   rw_pad[l], rb_pad[l][None, :], n_lat)

        block_e, n_used, row_tok, row_slot, n_out_rows = _route_plan(rec, cnt, n_tok)
        b1 = expert_b1[l]
        b1 = b1.reshape(b1.shape[0], -1, LANES, 2).transpose(0, 1, 3, 2).reshape(b1.shape[0], 1, -1)
        expert_out = _experts(h.reshape(n_tok * SLAB, LANES), block_e, n_used, row_tok, row_slot,
                              _w1_prep(expert_w1[l]), b1,
                              expert_w2[l].astype(BF16), expert_b2[l][:, None, :], n_out_rows)
        res = _combine(xn, expert_out, rec, mods, final_norm_w[None, :], n_lat, final)
        if final:
            out = res
        else:
            xc = res
    return out
```
